```python
import jax, jax.numpy as jnp
from jax import lax
import numpy as np

D_MODEL = 4096
BATCH = 4
SEQ = 4096
DEPTH = 1

MEM_LEN = 256
NOPE_DIM = 128
ROPE_DIM = 64
V_DIM = 128
MLA_HEADS = D_MODEL // V_DIM
Q_LORA = 1024
KV_LORA = 512
ROPE_THETA = 10000.0
Q_BLOCK = 128
CONV_DIM = D_MODEL
CONV_K = 3
MEM_HEADS = 4
MEM_HEAD_DIM = D_MODEL // MEM_HEADS
N_BRANCH = 3
BRANCH_WIDTH = D_MODEL
N_EXPERTS = 32
TOP_K = 4
D_EXPERT = 1536
SWIGLU_LIMIT = 7.0
SWIGLU_ALPHA = 1.702
EXPERT_BLOCK = 128
EPS = 1e-6

IN_SIZES = (Q_LORA, KV_LORA, ROPE_DIM, CONV_DIM, CONV_DIM, CONV_DIM,
            MEM_HEADS * MEM_HEAD_DIM, D_MODEL, D_MODEL, D_MODEL)
IN_COLS = Q_LORA + KV_LORA + ROPE_DIM + 3 * CONV_DIM + MEM_HEADS * MEM_HEAD_DIM + N_BRANCH * D_MODEL

kernel_name = "hybrid_mla_shortconv_memxattn_moe"


def rms_norm(x, g):
    xf = x.astype(jnp.float32)
    y = xf * lax.rsqrt(jnp.mean(xf * xf, axis=-1, keepdims=True) + EPS)
    return (y * g.astype(jnp.float32)).astype(x.dtype)


def rope_tables(positions):
    inv_freq = ROPE_THETA ** (-jnp.arange(0, ROPE_DIM, 2, dtype=jnp.float32) / ROPE_DIM)
    ang = positions.astype(jnp.float32)[..., None] * inv_freq
    return jnp.cos(ang), jnp.sin(ang)


def apply_rope(t, cos, sin):
    tf = t.astype(jnp.float32)
    t1, t2 = jnp.split(tf, 2, axis=-1)
    out = jnp.concatenate([t1 * cos - t2 * sin, t2 * cos + t1 * sin], axis=-1)
    return out.astype(t.dtype)


def mla_causal_attention(q_nope, q_rope, k_nope, k_rope, v):
    B, S, H, _ = q_nope.shape
    nb = S // Q_BLOCK
    scale = (NOPE_DIM + ROPE_DIM) ** -0.5
    key_pos = jnp.arange(S)

    def blocks(t):
        return jnp.swapaxes(t.reshape((B, nb, Q_BLOCK) + t.shape[2:]), 0, 1)

    def one_block(args):
        qn, qr, blk = args
        s = (jnp.einsum('bqhd,bkhd->bhqk', qn, k_nope, preferred_element_type=jnp.float32)
             + jnp.einsum('bqhr,bkr->bhqk', qr, k_rope, preferred_element_type=jnp.float32)) * scale
        q_pos = blk * Q_BLOCK + jnp.arange(Q_BLOCK)
        s = jnp.where(key_pos[None, :] <= q_pos[:, None], s, -jnp.inf)
        p = jax.nn.softmax(s, axis=-1).astype(v.dtype)
        return jnp.einsum('bhqk,bkhd->bqhd', p, v)

    o = lax.map(one_block, (blocks(q_nope), blocks(q_rope), jnp.arange(nb)))
    return jnp.swapaxes(o, 0, 1).reshape(B, S, H * V_DIM)


def short_conv_mixer(b_gate, c_gate, h, conv_w):
    u = c_gate * h
    y = lax.conv_general_dilated(
        u, conv_w[:, None, :].astype(u.dtype), window_strides=(1,),
        padding=[(CONV_K - 1, 0)], dimension_numbers=('NWC', 'WIO', 'NWC'),
        feature_group_count=u.shape[-1])
    return b_gate * y


def memory_cross_attention(q, mem_n, w_mem_kv, qn_g, kn_g):
    B, S, _ = q.shape
    M = mem_n.shape[1]
    k, v = jnp.split(mem_n @ w_mem_kv, 2, axis=-1)
    q = rms_norm(q.reshape(B, S, MEM_HEADS, MEM_HEAD_DIM), qn_g)
    k = rms_norm(k.reshape(B, M, MEM_HEADS, MEM_HEAD_DIM), kn_g)
    v = v.reshape(B, M, MEM_HEADS, MEM_HEAD_DIM)
    s = jnp.einsum('bshd,bmhd->bhsm', q, k, preferred_element_type=jnp.float32) * MEM_HEAD_DIM ** -0.5
    p = jax.nn.softmax(s, axis=-1).astype(v.dtype)
    return jnp.einsum('bhsm,bmhd->bshd', p, v).reshape(B, S, MEM_HEADS * MEM_HEAD_DIM)


def routed_swiglu_moe(h, w_router, b_router, w_gate, b_gate, w_up, b_up, w_down, b_down, layer):
    B, S, D = h.shape
    N = B * S
    xt = h.reshape(N, D)
    logits = jnp.dot(xt, w_router, preferred_element_type=jnp.float32) + b_router.astype(jnp.float32)
    top_vals, top_idx = lax.top_k(logits, TOP_K)
    top_w = jax.nn.softmax(top_vals, axis=-1)
    A = N * TOP_K
    flat_e = top_idx.reshape(A).astype(jnp.int32)
    flat_tok = jnp.arange(A, dtype=jnp.int32) // TOP_K
    flat_w = top_w.reshape(A)
    order = jnp.argsort(flat_e, stable=True)
    sorted_e, sorted_tok, sorted_w = flat_e[order], flat_tok[order], flat_w[order]
    counts = jnp.bincount(flat_e, length=N_EXPERTS).astype(jnp.int32)
    starts = jnp.cumsum(counts) - counts
    padded = ((counts + EXPERT_BLOCK - 1) // EXPERT_BLOCK) * EXPERT_BLOCK
    pends = jnp.cumsum(padded)
    pstarts = pends - padded
    dest = pstarts[sorted_e] + (jnp.arange(A, dtype=jnp.int32) - starts[sorted_e])
    n_blocks = -(-A // EXPERT_BLOCK) + N_EXPERTS
    P = n_blocks * EXPERT_BLOCK
    slot_tok = jnp.full((P,), N, jnp.int32).at[dest].set(sorted_tok)
    slot_w = jnp.zeros((P,), jnp.float32).at[dest].set(sorted_w)
    block_start = jnp.arange(n_blocks, dtype=jnp.int32) * EXPERT_BLOCK
    block_e = jnp.clip(jnp.searchsorted(pends, block_start, side='right'), 0, N_EXPERTS - 1)
    x_pad = jnp.concatenate([xt, jnp.zeros((1, D), xt.dtype)], axis=0)

    def expert_block(args):
        tok, wt, e = args
        xb = x_pad[tok]
        g = xb @ w_gate[layer, e] + b_gate[layer, e]
        u = xb @ w_up[layer, e] + b_up[layer, e]
        g = jnp.minimum(g, SWIGLU_LIMIT)
        u = jnp.clip(u, -SWIGLU_LIMIT, SWIGLU_LIMIT)
        act = g * jax.nn.sigmoid(SWIGLU_ALPHA * g) * (u + 1.0)
        return (act @ w_down[layer, e] + b_down[layer, e]) * wt[:, None].astype(xb.dtype)

    ys = lax.map(expert_block, (slot_tok.reshape(n_blocks, EXPERT_BLOCK),
                                slot_w.reshape(n_blocks, EXPERT_BLOCK), block_e))
    out = jnp.zeros((N + 1, D), ys.dtype).at[slot_tok].add(ys.reshape(P, D))[:N]
    return out.reshape(B, S, D)


def setup_inputs(seed: int = 0) -> dict:
    key = jax.random.key(seed)
    ks = iter(jax.random.split(key, 40))

    def normal(shape, scale):
        return jax.random.normal(next(ks), shape, jnp.float32) * scale

    def gain(shape):
        return 1.0 + 0.02 * jax.random.normal(next(ks), shape, jnp.float32)

    L = DEPTH
    x = normal((BATCH, SEQ, D_MODEL), 1.0)
    mem = normal((BATCH, MEM_LEN, D_MODEL), 1.0)
    offsets = jax.random.randint(next(ks), (BATCH, 1), 0, 1024, dtype=jnp.int32)
    positions = (offsets + jnp.arange(SEQ, dtype=jnp.int32)[None, :]).astype(jnp.int32)
    return {
        "x": x,
        "mem": mem,
        "positions": positions,
        "norm1_g": gain((L, D_MODEL)),
        "w_in": normal((L, D_MODEL, IN_COLS), D_MODEL ** -0.5),
        "q_lat_g": gain((L, Q_LORA)),
        "kv_lat_g": gain((L, KV_LORA)),
        "w_uq": normal((L, Q_LORA, MLA_HEADS * (NOPE_DIM + ROPE_DIM)), Q_LORA ** -0.5),
        "w_ukv": normal((L, KV_LORA, MLA_HEADS * (NOPE_DIM + V_DIM)), KV_LORA ** -0.5),
        "qn_nope_g": gain((L, NOPE_DIM)),
        "qn_rope_g": gain((L, ROPE_DIM)),
        "kn_nope_g": gain((L, NOPE_DIM)),
        "kn_rope_g": gain((L, ROPE_DIM)),
        "conv_w": normal((L, CONV_K, CONV_DIM), CONV_K ** -0.5),
        "mem_norm_g": gain((L, D_MODEL)),
        "w_mem_kv": normal((L, D_MODEL, 2 * MEM_HEADS * MEM_HEAD_DIM), D_MODEL ** -0.5),
        "mem_qn_g": gain((L, MEM_HEAD_DIM)),
        "mem_kn_g": gain((L, MEM_HEAD_DIM)),
        "w_branch": normal((L, N_BRANCH, BRANCH_WIDTH, D_MODEL), BRANCH_WIDTH ** -0.5),
        "w_out": normal((L, D_MODEL, D_MODEL), D_MODEL ** -0.5),
        "norm2_g": gain((L, D_MODEL)),
        "w_router": normal((L, D_MODEL, N_EXPERTS), D_MODEL ** -0.5),
        "b_router": normal((L, N_EXPERTS), 0.01),
        "w_gate": normal((L, N_EXPERTS, D_MODEL, D_EXPERT), D_MODEL ** -0.5),
        "b_gate": normal((L, N_EXPERTS, D_EXPERT), 0.02),
        "w_up": normal((L, N_EXPERTS, D_MODEL, D_EXPERT), D_MODEL ** -0.5),
        "b_up": normal((L, N_EXPERTS, D_EXPERT), 0.02),
        "w_down": normal((L, N_EXPERTS, D_EXPERT, D_MODEL), D_EXPERT ** -0.5),
        "b_down": normal((L, N_EXPERTS, D_MODEL), 0.02),
    }


def reference(x, mem, positions, norm1_g, w_in, q_lat_g, kv_lat_g, w_uq, w_ukv,
              qn_nope_g, qn_rope_g, kn_nope_g, kn_rope_g, conv_w, mem_norm_g, w_mem_kv,
              mem_qn_g, mem_kn_g, w_branch, w_out, norm2_g, w_router, b_router,
              w_gate, b_gate, w_up, b_up, w_down, b_down):
    B, S, _ = x.shape
    H = MLA_HEADS
    cuts = [int(c) for c in np.cumsum(IN_SIZES)[:-1]]
    cos, sin = rope_tables(positions)
    cos_h, sin_h = cos[:, :, None, :], sin[:, :, None, :]
    for l in range(DEPTH):
        xn = rms_norm(x, norm1_g[l])
        proj = xn @ w_in[l]
        c_q, c_kv, k_rope, cb, cc, ch, mq, g_mla, g_conv, g_mem = jnp.split(proj, cuts, axis=-1)

        q = (rms_norm(c_q, q_lat_g[l]) @ w_uq[l]).reshape(B, S, H, NOPE_DIM + ROPE_DIM)
        kv = (rms_norm(c_kv, kv_lat_g[l]) @ w_ukv[l]).reshape(B, S, H, NOPE_DIM + V_DIM)
        q_nope = rms_norm(q[..., :NOPE_DIM], qn_nope_g[l])
        q_rope = apply_rope(rms_norm(q[..., NOPE_DIM:], qn_rope_g[l]), cos_h, sin_h)
        k_nope = rms_norm(kv[..., :NOPE_DIM], kn_nope_g[l])
        v = kv[..., NOPE_DIM:]
        k_rope = apply_rope(rms_norm(k_rope, kn_rope_g[l]), cos, sin)
        a_mla = mla_causal_attention(q_nope, q_rope, k_nope, k_rope, v)

        a_conv = short_conv_mixer(cb, cc, ch, conv_w[l])

        a_mem = memory_cross_attention(mq, rms_norm(mem, mem_norm_g[l]), w_mem_kv[l],
                                       mem_qn_g[l], mem_kn_g[l])

        merged = (jax.nn.sigmoid(g_mla) * (a_mla @ w_branch[l, 0])
                  + jax.nn.sigmoid(g_conv) * (a_conv @ w_branch[l, 1])
                  + jax.nn.sigmoid(g_mem) * (a_mem @ w_branch[l, 2]))
        x = x + merged @ w_out[l]

        x = x + routed_swiglu_moe(rms_norm(x, norm2_g[l]), w_router[l], b_router[l],
                                  w_gate, b_gate, w_up, b_up, w_down, b_down, l)
    return x
```

```python
import functools

import jax
import jax.numpy as jnp
from jax import lax
from jax.experimental import pallas as pl
from jax.experimental.pallas import tpu as pltpu

F32 = jnp.float32
BF16 = jnp.bfloat16
U32 = jnp.uint32
I32 = jnp.int32

EPS = 1e-6
NOPE_DIM = 128
ROPE_DIM = 64
V_DIM = 128
QK_DIM = NOPE_DIM + ROPE_DIM
HEAD_COLS = 256
MEM_HEADS = 4
TOP_K = 4
ROPE_THETA = 10000.0
SWIGLU_LIMIT = 7.0
SWIGLU_ALPHA = 1.702
LANES = 128
V7X_VMEM_LIMIT_BYTES = 56 * 1024 * 1024
HI_MASK = 0xFFFF0000


def _params(*sem):
    return pltpu.CompilerParams(dimension_semantics=sem, vmem_limit_bytes=V7X_VMEM_LIMIT_BYTES)


def _tile(n, want):
    t = min(n, want)
    while n % t:
        t //= 2
    assert t >= 1
    return t


def _bits(x):
    return lax.bitcast_convert_type(x, U32)


def _pack_halves(y):
    half = y.shape[1] // 2
    yb = y.astype(BF16).astype(F32)
    lo = _bits(yb[:, :half]) >> 16
    hi = _bits(yb[:, half:]) & jnp.uint32(HI_MASK)
    return lo | hi


def _unpack_halves(p):
    lo = lax.bitcast_convert_type(p << 16, F32)
    hi = lax.bitcast_convert_type(p & jnp.uint32(HI_MASK), F32)
    return lo, hi


def _rmsnorm_kernel(x_ref, g_ref, o_ref):
    x = x_ref[...].astype(F32)
    r = lax.rsqrt(jnp.mean(x * x, axis=-1, keepdims=True) + EPS)
    o_ref[...] = (x * r * g_ref[...]).astype(o_ref.dtype)


def _rmsnorm(x, g, tm=512):
    m, d = x.shape
    tm = _tile(m, tm)
    return pl.pallas_call(
        _rmsnorm_kernel,
        grid=(m // tm,),
        in_specs=[pl.BlockSpec((tm, d), lambda i: (i, 0)), pl.BlockSpec((1, d), lambda i: (0, 0))],
        out_specs=pl.BlockSpec((tm, d), lambda i: (i, 0)),
        out_shape=jax.ShapeDtypeStruct((m, d), BF16),
        compiler_params=_params("parallel"),
        name="rmsnorm",
    )(x, g.reshape(1, d))


def _mm_kernel(x_ref, w_ref, o_ref):
    o_ref[...] = jnp.dot(x_ref[...], w_ref[...], preferred_element_type=F32).astype(o_ref.dtype)


def _mm_res_kernel(x_ref, w_ref, r_ref, o_ref):
    o_ref[...] = r_ref[...] + jnp.dot(x_ref[...], w_ref[...], preferred_element_type=F32)


def _matmul(x, w, out_dtype, tm=1024, tn=1024, res=None, name="matmul"):
    m, k = x.shape
    n = w.shape[1]
    tm, tn = _tile(m, tm), _tile(n, tn)
    in_specs = [pl.BlockSpec((tm, k), lambda i, j: (i, 0)), pl.BlockSpec((k, tn), lambda i, j: (0, j))]
    args = [x, w]
    body = _mm_kernel
    if res is not None:
        in_specs.append(pl.BlockSpec((tm, tn), lambda i, j: (i, j)))
        args.append(res)
        body = _mm_res_kernel
    return pl.pallas_call(
        body,
        grid=(m // tm, n // tn),
        in_specs=in_specs,
        out_specs=pl.BlockSpec((tm, tn), lambda i, j: (i, j)),
        out_shape=jax.ShapeDtypeStruct((m, n), out_dtype),
        compiler_params=_params("parallel", "parallel"),
        name=name,
    )(*args)


def _rope_tables(pos_ref, invf_ref, sgn_ref, cos_s, sin_s):
    ang = pos_ref[...].astype(F32) * invf_ref[...]
    cos_s[...] = jnp.cos(ang)
    sin_s[...] = jnp.sin(ang) * sgn_ref[...]


def _rope_pair(slab, gain, cos, sin):
    r = lax.rsqrt(jnp.mean(slab * slab, axis=-1, keepdims=True) + EPS)
    slab = slab * r * gain
    return slab * cos + pltpu.roll(slab, ROPE_DIM, 1) * sin


def _qproj_kernel(cq_ref, pos_ref, invf_ref, sgn_ref, glat_ref, gq_ref, w_ref, o_ref, cqn_s, cos_s, sin_s, *, scale):
    @pl.when(pl.program_id(1) == 0)
    def _():
        c = cq_ref[...].astype(F32)
        r = lax.rsqrt(jnp.mean(c * c, axis=-1, keepdims=True) + EPS)
        cqn_s[...] = (c * r * glat_ref[...]).astype(BF16)
        _rope_tables(pos_ref, invf_ref, sgn_ref, cos_s, sin_s)

    q = jnp.dot(cqn_s[...], w_ref[...], preferred_element_type=F32)
    nope = q[:, :NOPE_DIM]
    rn = lax.rsqrt(jnp.mean(nope * nope, axis=-1, keepdims=True) + EPS)
    nope = nope * rn * gq_ref[:, :NOPE_DIM]
    rot = _rope_pair(q[:, NOPE_DIM:], gq_ref[:, NOPE_DIM:], cos_s[...], sin_s[...])
    o_ref[:, :NOPE_DIM] = (nope * scale).astype(o_ref.dtype)
    o_ref[:, NOPE_DIM:] = (rot[:, :ROPE_DIM] * scale).astype(o_ref.dtype)


def _kvproj_kernel(ckv_ref, kr_ref, pos_ref, invf_ref, sgn_ref, glat_ref, gkr_ref, gkn_ref, w_ref,
                   k_ref, v_ref, ckvn_s, kr_s, cos_s, sin_s):
    @pl.when(pl.program_id(1) == 0)
    def _():
        c = ckv_ref[...].astype(F32)
        r = lax.rsqrt(jnp.mean(c * c, axis=-1, keepdims=True) + EPS)
        ckvn_s[...] = (c * r * glat_ref[...]).astype(BF16)
        _rope_tables(pos_ref, invf_ref, sgn_ref, cos_s, sin_s)
        kr_s[...] = _rope_pair(kr_ref[...].astype(F32), gkr_ref[...], cos_s[...], sin_s[...]).astype(BF16)

    kv = jnp.dot(ckvn_s[...], w_ref[...], preferred_element_type=F32)
    kn = kv[:, :NOPE_DIM]
    rn = lax.rsqrt(jnp.mean(kn * kn, axis=-1, keepdims=True) + EPS)
    k_ref[:, :NOPE_DIM] = (kn * rn * gkn_ref[...]).astype(k_ref.dtype)
    k_ref[:, NOPE_DIM:] = kr_s[:, :ROPE_DIM]
    v_ref[...] = kv[:, NOPE_DIM:].astype(v_ref.dtype)


def _rope_consts():
    inv_freq = ROPE_THETA ** (-jnp.arange(0, ROPE_DIM, 2, dtype=F32) / ROPE_DIM)
    invf = jnp.tile(inv_freq, LANES // (ROPE_DIM // 2)).reshape(1, LANES)
    half = jnp.concatenate([-jnp.ones((ROPE_DIM // 2,), F32), jnp.ones((ROPE_DIM // 2,), F32)])
    sgn = jnp.tile(half, LANES // ROPE_DIM).reshape(1, LANES)
    return invf, sgn


def _swap_halves(g):
    return jnp.concatenate([g[..., ROPE_DIM // 2:], g[..., :ROPE_DIM // 2]], axis=-1)


def _mla_projections(small, pos, b, s, heads, q_lora, kv_lora, q_lat_g, kv_lat_g, wq, wkv,
                     qn_nope_g, qn_rope_g, kn_nope_g, kn_rope_g, tm=512):
    n = b * s
    tm = _tile(s, tm)
    spb = s // tm
    invf, sgn = _rope_consts()
    scale = float(QK_DIM) ** -0.5
    gq = jnp.concatenate([qn_nope_g, qn_rope_g, _swap_halves(qn_rope_g)]).reshape(1, HEAD_COLS)
    gkr = jnp.concatenate([kn_rope_g, _swap_halves(kn_rope_g)]).reshape(1, LANES)
    row = lambda i, h: (i, 0)
    const = lambda i, h: (0, 0)
    head_out = lambda i, h: (i // spb, h, i % spb, 0)
    assert q_lora % kv_lora == 0 and kv_lora % LANES == 0
    ckv_blk = q_lora // kv_lora
    kr_blk = (q_lora + kv_lora) // LANES

    q = pl.pallas_call(
        functools.partial(_qproj_kernel, scale=scale),
        grid=(n // tm, heads),
        in_specs=[pl.BlockSpec((tm, q_lora), row), pl.BlockSpec((tm, 1), row),
                  pl.BlockSpec((1, LANES), const), pl.BlockSpec((1, LANES), const),
                  pl.BlockSpec((1, q_lora), const), pl.BlockSpec((1, HEAD_COLS), const),
                  pl.BlockSpec((q_lora, HEAD_COLS), lambda i, h: (0, h))],
        out_specs=pl.BlockSpec((None, None, tm, QK_DIM), head_out),
        out_shape=jax.ShapeDtypeStruct((b, heads, s, QK_DIM), BF16),
        scratch_shapes=[pltpu.VMEM((tm, q_lora), BF16), pltpu.VMEM((tm, LANES), F32), pltpu.VMEM((tm, LANES), F32)],
        compiler_params=_params("parallel", "arbitrary"),
        name="mla_q_proj",
    )(small, pos, invf, sgn, q_lat_g.reshape(1, q_lora), gq, wq)

    k, v = pl.pallas_call(
        _kvproj_kernel,
        grid=(n // tm, heads),
        in_specs=[pl.BlockSpec((tm, kv_lora), lambda i, h: (i, ckv_blk)),
                  pl.BlockSpec((tm, LANES), lambda i, h: (i, kr_blk)),
                  pl.BlockSpec((tm, 1), row),
                  pl.BlockSpec((1, LANES), const), pl.BlockSpec((1, LANES), const),
                  pl.BlockSpec((1, kv_lora), const), pl.BlockSpec((1, LANES), const),
                  pl.BlockSpec((1, NOPE_DIM), const),
                  pl.BlockSpec((kv_lora, HEAD_COLS), lambda i, h: (0, h))],
        out_specs=[pl.BlockSpec((None, None, tm, QK_DIM), head_out),
                   pl.BlockSpec((None, None, tm, V_DIM), head_out)],
        out_shape=[jax.ShapeDtypeStruct((b, heads, s, QK_DIM), BF16),
                   jax.ShapeDtypeStruct((b, heads, s, V_DIM), BF16)],
        scratch_shapes=[pltpu.VMEM((tm, kv_lora), BF16), pltpu.VMEM((tm, LANES), BF16),
                        pltpu.VMEM((tm, LANES), F32), pltpu.VMEM((tm, LANES), F32)],
        compiler_params=_params("parallel", "arbitrary"),
        name="mla_kv_proj",
    )(small, small, pos, invf, sgn, kv_lat_g.reshape(1, kv_lora), gkr, kn_nope_g.reshape(1, NOPE_DIM), wkv)
    return q, k, v


def _flash_kernel(q_ref, k_ref, v_ref, o_ref, *, tq, tk):
    i = pl.program_id(2)
    q = q_ref[...]
    n_diag = tq // tk
    n_full = i * n_diag

    def step(j, carry, masked):
        m, l, acc = carry
        start = pl.multiple_of(j * tk, tk)
        ks = k_ref[pl.ds(start, tk), :]
        vs = v_ref[pl.ds(start, tk), :]
        s = lax.dot_general(q, ks, (((1,), (1,)), ((), ())), preferred_element_type=F32)
        if masked:
            row = i * tq + lax.broadcasted_iota(I32, (tq, tk), 0)
            col = j * tk + lax.broadcasted_iota(I32, (tq, tk), 1)
            s = jnp.where(col <= row, s, -jnp.inf)
        m_new = jnp.maximum(m, jnp.max(s, axis=-1, keepdims=True))
        alpha = jnp.exp(m - m_new)
        p = jnp.exp(s - m_new)
        l = alpha * l + jnp.sum(p, axis=-1, keepdims=True)
        acc = alpha * acc + jnp.dot(p.astype(BF16), vs, preferred_element_type=F32)
        return m_new, l, acc

    init = (jnp.full((tq, 1), -jnp.inf, F32), jnp.zeros((tq, 1), F32), jnp.zeros((tq, V_DIM), F32))
    carry = lax.fori_loop(0, n_full, functools.partial(step, masked=False), init)
    for d in range(n_diag):
        carry = step(n_full + d, carry, True)
    _, l, acc = carry
    o_ref[...] = (acc / l).astype(o_ref.dtype)


def _flash_attention(q, k, v, tq=512, tk=512):
    b, heads, s, _ = q.shape
    tq = _tile(s, tq)
    tk = _tile(tq, tk)
    spb = s // tq
    return pl.pallas_call(
        functools.partial(_flash_kernel, tq=tq, tk=tk),
        grid=(b, heads, spb),
        in_specs=[pl.BlockSpec((None, None, tq, QK_DIM), lambda bi, h, i: (bi, h, i, 0)),
                  pl.BlockSpec((None, None, s, QK_DIM), lambda bi, h, i: (bi, h, 0, 0)),
                  pl.BlockSpec((None, None, s, V_DIM), lambda bi, h, i: (bi, h, 0, 0))],
        out_specs=pl.BlockSpec((tq, V_DIM), lambda bi, h, i: (bi * spb + i, h)),
        out_shape=jax.ShapeDtypeStruct((b * s, heads * V_DIM), BF16),
        compiler_params=_params("parallel", "parallel", "arbitrary"),
        name="mla_flash_attention",
    )(q, k, v)


def _conv_kernel(cb_ref, cc_ref, ch_ref, pc_ref, ph_ref, w_ref, o_ref, *, tiles_per_seq):
    first = (pl.program_id(0) % tiles_per_seq) == 0
    u = cc_ref[...].astype(F32) * ch_ref[...].astype(F32)
    halo = pc_ref[...].astype(F32) * ph_ref[...].astype(F32)
    halo = jnp.where(first, 0.0, halo)
    cat = jnp.concatenate([halo, u], axis=0)
    n_halo = halo.shape[0]
    u1 = pltpu.roll(cat, 1, 0)[n_halo:]
    u2 = pltpu.roll(cat, 2, 0)[n_halo:]
    y = w_ref[0:1, :] * u2 + w_ref[1:2, :] * u1 + w_ref[2:3, :] * u
    o_ref[...] = (cb_ref[...].astype(F32) * y).astype(o_ref.dtype)


def _short_conv(big, conv_w, n, s, d, ts=512, tc=1024):
    ts, tc = _tile(s, ts), _tile(d, tc)
    halo = 8
    cpb = d // tc
    seg = lambda k: (lambda i, j: (i, k * cpb + j))
    prev = lambda k: (lambda i, j: (jnp.maximum(i * (ts // halo) - 1, 0), k * cpb + j))
    return pl.pallas_call(
        functools.partial(_conv_kernel, tiles_per_seq=s // ts),
        grid=(n // ts, cpb),
        in_specs=[pl.BlockSpec((ts, tc), seg(0)), pl.BlockSpec((ts, tc), seg(1)), pl.BlockSpec((ts, tc), seg(2)),
                  pl.BlockSpec((halo, tc), prev(1)), pl.BlockSpec((halo, tc), prev(2)),
                  pl.BlockSpec((conv_w.shape[0], tc), lambda i, j: (0, j))],
        out_specs=pl.BlockSpec((ts, tc), lambda i, j: (i, j)),
        out_shape=jax.ShapeDtypeStruct((n, d), BF16),
        compiler_params=_params("parallel", "parallel"),
        name="short_conv",
    )(big, big, big, big, big, conv_w)


def _memattn_kernel(q_ref, k_ref, v_ref, gq_ref, gk_ref, o_ref, kn_s, *, scale):
    @pl.when(pl.program_id(2) == 0)
    def _():
        k = k_ref[...].astype(F32)
        r = lax.rsqrt(jnp.mean(k * k, axis=-1, keepdims=True) + EPS)
        kn_s[...] = (k * r * gk_ref[...]).astype(BF16)

    q = q_ref[...].astype(F32)
    r = lax.rsqrt(jnp.mean(q * q, axis=-1, keepdims=True) + EPS)
    qn = (q * (r * scale) * gq_ref[...]).astype(BF16)
    s = lax.dot_general(qn, kn_s[...], (((1,), (1,)), ((), ())), preferred_element_type=F32)
    m = jnp.max(s, axis=-1, keepdims=True)
    p = jnp.exp(s - m)
    l = jnp.sum(p, axis=-1, keepdims=True)
    o = jnp.dot(p.astype(BF16), v_ref[...], preferred_element_type=F32)
    o_ref[...] = (o / l).astype(o_ref.dtype)


def _memory_attention(big, kvm, gq, gk, b, s, d, mem_len, q_seg, ts=512):
    hd = d // MEM_HEADS
    ts = _tile(s, ts)
    spb = s // ts
    return pl.pallas_call(
        functools.partial(_memattn_kernel, scale=float(hd) ** -0.5),
        grid=(b, MEM_HEADS, spb),
        in_specs=[pl.BlockSpec((ts, hd), lambda bi, h, i: (bi * spb + i, q_seg * MEM_HEADS + h)),
                  pl.BlockSpec((mem_len, hd), lambda bi, h, i: (bi, h)),
                  pl.BlockSpec((mem_len, hd), lambda bi, h, i: (bi, MEM_HEADS + h)),
                  pl.BlockSpec((1, hd), lambda bi, h, i: (0, 0)),
                  pl.BlockSpec((1, hd), lambda bi, h, i: (0, 0))],
        out_specs=pl.BlockSpec((ts, hd), lambda bi, h, i: (bi * spb + i, h)),
        out_shape=jax.ShapeDtypeStruct((b * s, d), BF16),
        scratch_shapes=[pltpu.VMEM((mem_len, hd), BF16)],
        compiler_params=_params("parallel", "parallel", "arbitrary"),
        name="memory_attention",
    )(big, kvm, kvm, gq.reshape(1, hd), gk.reshape(1, hd))


def _merge_kernel(a0_ref, a1_ref, a2_ref, w_ref, g_ref, o_ref, acc_s):
    br = pl.program_id(2)
    gate = jax.nn.sigmoid(g_ref[...].astype(F32))

    @pl.when(br == 0)
    def _():
        acc_s[...] = gate * jnp.dot(a0_ref[...], w_ref[...], preferred_element_type=F32)

    @pl.when(br == 1)
    def _():
        acc_s[...] += gate * jnp.dot(a1_ref[...], w_ref[...], preferred_element_type=F32)

    @pl.when(br == 2)
    def _():
        o_ref[...] = (acc_s[...] + gate * jnp.dot(a2_ref[...], w_ref[...], preferred_element_type=F32)).astype(o_ref.dtype)


def _gated_merge(a_mla, a_conv, a_mem, w_branch, big, gate_seg, tm=512, tn=1024):
    n, d = a_mla.shape
    tm, tn = _tile(n, tm), _tile(d, tn)
    cpb = d // tn
    lhs = pl.BlockSpec((tm, d), lambda i, j, br: (i, 0))
    return pl.pallas_call(
        _merge_kernel,
        grid=(n // tm, cpb, 3),
        in_specs=[lhs, lhs, lhs,
                  pl.BlockSpec((None, d, tn), lambda i, j, br: (br, 0, j)),
                  pl.BlockSpec((tm, tn), lambda i, j, br: (i, (gate_seg + br) * cpb + j))],
        out_specs=pl.BlockSpec((tm, tn), lambda i, j, br: (i, j)),
        out_shape=jax.ShapeDtypeStruct((n, d), BF16),
        scratch_shapes=[pltpu.VMEM((tm, tn), F32)],
        compiler_params=_params("parallel", "parallel", "arbitrary"),
        name="gated_merge",
    )(a_mla, a_conv, a_mem, w_branch, big)


def _router_kernel(x_ref, g_ref, wr_ref, br_ref, hp_ref, idx_ref, wt_ref, rank_ref, cnt_ref, carry_s, *, n_exp):
    tm = x_ref.shape[0]

    @pl.when(pl.program_id(0) == 0)
    def _():
        carry_s[...] = jnp.zeros_like(carry_s)

    x = x_ref[...]
    r = lax.rsqrt(jnp.mean(x * x, axis=-1, keepdims=True) + EPS)
    h = x * r * g_ref[...]
    hp_ref[...] = _pack_halves(h)

    logits = jnp.dot(h, wr_ref[...], preferred_element_type=F32, precision=lax.Precision.HIGHEST) + br_ref[...]
    lane = lax.broadcasted_iota(I32, (tm, n_exp), 1)
    out_lane = lax.broadcasted_iota(I32, (tm, LANES), 1)
    work = logits
    vals, hots = [], []
    idx_out = jnp.zeros((tm, LANES), I32)
    for k in range(TOP_K):
        m = jnp.max(work, axis=-1, keepdims=True)
        idx = jnp.min(jnp.where(work == m, lane, n_exp), axis=-1, keepdims=True)
        hot = lane == idx
        vals.append(m)
        hots.append(hot)
        idx_out = jnp.where(out_lane == k, idx, idx_out)
        work = jnp.where(hot, -jnp.inf, work)
    idx_ref[...] = idx_out

    exps = [jnp.exp(v - vals[0]) for v in vals]
    denom = exps[0]
    for e in exps[1:]:
        denom = denom + e
    wt_out = jnp.zeros((tm, LANES), F32)
    for k in range(TOP_K):
        wt_out = jnp.where(out_lane == k, exps[k] / denom, wt_out)
    wt_ref[...] = wt_out

    member = jnp.zeros((tm, n_exp), F32)
    for hot in hots:
        member = member + hot.astype(F32)
    tri = (lax.broadcasted_iota(I32, (tm, tm), 1) < lax.broadcasted_iota(I32, (tm, tm), 0)).astype(BF16)
    prefix = jnp.dot(tri, member.astype(BF16), preferred_element_type=F32) + carry_s[0:1, 0:n_exp]
    rank_out = jnp.zeros((tm, LANES), I32)
    for k in range(TOP_K):
        rk = jnp.sum(jnp.where(hots[k], prefix, 0.0), axis=-1, keepdims=True).astype(I32)
        rank_out = jnp.where(out_lane == k, rk, rank_out)
    rank_ref[...] = rank_out

    carry_s[0:1, 0:n_exp] = carry_s[0:1, 0:n_exp] + jnp.sum(member, axis=0, keepdims=True)
    cnt_ref[...] = carry_s[...]


def _router(x1, g, w_router, b_router, tm=512):
    n, d = x1.shape
    n_exp = w_router.shape[1]
    assert n_exp <= LANES
    tm = _tile(n, tm)
    row = lambda i: (i, 0)
    const = lambda i: (0, 0)
    return pl.pallas_call(
        functools.partial(_router_kernel, n_exp=n_exp),
        grid=(n // tm,),
        in_specs=[pl.BlockSpec((tm, d), row), pl.BlockSpec((1, d), const),
                  pl.BlockSpec((d, n_exp), const), pl.BlockSpec((1, n_exp), const)],
        out_specs=[pl.BlockSpec((tm, d // 2), row), pl.BlockSpec((tm, LANES), row), pl.BlockSpec((tm, LANES), row),
                   pl.BlockSpec((tm, LANES), row), pl.BlockSpec((8, LANES), const)],
        out_shape=[jax.ShapeDtypeStruct((n, d // 2), U32), jax.ShapeDtypeStruct((n, LANES), I32),
                   jax.ShapeDtypeStruct((n, LANES), F32), jax.ShapeDtypeStruct((n, LANES), I32),
                   jax.ShapeDtypeStruct((8, LANES), F32)],
        scratch_shapes=[pltpu.VMEM((8, LANES), F32)],
        compiler_params=_params("arbitrary"),
        name="router",
    )(x1, g.reshape(1, d), w_router, b_router.reshape(1, n_exp))


def _dispatch_kernel(dest_ref, zflag_ref, hp_ref, xs_ref, zeros_s, sem, zsem, *, rows):
    base = pl.program_id(0) * rows
    te = zeros_s.shape[0]

    @pl.when(pl.program_id(0) == 0)
    def _():
        zeros_s[...] = jnp.zeros_like(zeros_s)

        def fill(t):
            return pltpu.make_async_copy(zeros_s, xs_ref.at[pl.ds(pl.multiple_of(t * te, te), te)], zsem)

        def start(t, c):
            @pl.when(zflag_ref[t] != 0)
            def _():
                fill(t).start()
            return c

        def finish(t, c):
            @pl.when(zflag_ref[t] != 0)
            def _():
                fill(t).wait()
            return c

        lax.fori_loop(0, zflag_ref.shape[0], start, 0)
        lax.fori_loop(0, zflag_ref.shape[0], finish, 0)

    def issue(r, c):
        for k in range(TOP_K):
            d = dest_ref[(base + r) * TOP_K + k]
            pltpu.make_async_copy(hp_ref.at[pl.ds(base + r, 1)], xs_ref.at[pl.ds(d, 1)], sem).start()
        return c

    lax.fori_loop(0, rows, issue, 0)

    def drain(r, c):
        for k in range(TOP_K):
            pltpu.make_async_copy(hp_ref.at[pl.ds(0, 1)], xs_ref.at[pl.ds(0, 1)], sem).wait()
        return c

    lax.fori_loop(0, rows, drain, 0)


def _dispatch(dest_flat, zero_flag, hp, te, rows=256):
    n, half = hp.shape
    rows = _tile(n, rows)
    n_slots = zero_flag.shape[0] * te
    return pl.pallas_call(
        functools.partial(_dispatch_kernel, rows=rows),
        grid_spec=pltpu.PrefetchScalarGridSpec(
            num_scalar_prefetch=2,
            grid=(n // rows,),
            in_specs=[pl.BlockSpec(memory_space=pl.ANY)],
            out_specs=pl.BlockSpec(memory_space=pl.ANY),
            scratch_shapes=[pltpu.VMEM((te, half), U32), pltpu.SemaphoreType.DMA(()), pltpu.SemaphoreType.DMA(())],
        ),
        out_shape=jax.ShapeDtypeStruct((n_slots, half), U32),
        compiler_params=_params("arbitrary"),
        name="expert_dispatch",
    )(dest_flat, zero_flag, hp)


def _expert_kernel(te_ref, nu_ref, xs_ref, wg_ref, bg_ref, wu_ref, bu_ref, wd_ref, bd_ref, o_ref, xb_s, acc_s):
    t = pl.program_id(0)
    c = pl.program_id(1)
    last = pl.num_programs(1) - 1
    half = xs_ref.shape[1]

    @pl.when(t < nu_ref[0])
    def _():
        @pl.when(c == 0)
        def _():
            lo, hi = _unpack_halves(xs_ref[...])
            xb_s[:, :half] = lo.astype(BF16)
            xb_s[:, half:] = hi.astype(BF16)

        xb = xb_s[...]
        g = jnp.dot(xb, wg_ref[...], preferred_element_type=F32) + bg_ref[...]
        u = jnp.dot(xb, wu_ref[...], preferred_element_type=F32) + bu_ref[...]
        g = jnp.minimum(g, SWIGLU_LIMIT)
        u = jnp.clip(u, -SWIGLU_LIMIT, SWIGLU_LIMIT)
        act = (g * jax.nn.sigmoid(SWIGLU_ALPHA * g) * (u + 1.0)).astype(BF16)
        part = jnp.dot(act, wd_ref[...], preferred_element_type=F32)

        @pl.when(c == 0)
        def _():
            acc_s[...] = part + bd_ref[...]

        @pl.when(jnp.logical_and(c > 0, c < last))
        def _():
            acc_s[...] += part

        @pl.when(c == last)
        def _():
            o_ref[...] = _pack_halves(acc_s[...] + part)

    @pl.when(jnp.logical_and(t >= nu_ref[0], c == last))
    def _():
        o_ref[...] = jnp.zeros_like(o_ref)


def _expert_mlp(tile_e, n_used, xs, wg, bg, wu, bu, wd, bd, te, tce=256):
    n_slots, half = xs.shape
    d = 2 * half
    n_exp, _, de = wg.shape
    tce = _tile(de, tce)
    nc = de // tce
    assert nc >= 2
    n_tiles = n_slots // te

    def live(t, c, te_ref, nu_ref):
        tt = jnp.minimum(t, nu_ref[0] - 1)
        cc = jnp.where(t < nu_ref[0], c, nc - 1)
        return tt, cc, te_ref[tt]

    def xs_map(t, c, te_ref, nu_ref):
        return live(t, c, te_ref, nu_ref)[0], 0

    def up_map(t, c, te_ref, nu_ref):
        _, cc, e = live(t, c, te_ref, nu_ref)
        return e, 0, cc

    def down_map(t, c, te_ref, nu_ref):
        _, cc, e = live(t, c, te_ref, nu_ref)
        return e, cc, 0

    def bd_map(t, c, te_ref, nu_ref):
        return live(t, c, te_ref, nu_ref)[2], 0, 0

    return pl.pallas_call(
        _expert_kernel,
        grid_spec=pltpu.PrefetchScalarGridSpec(
            num_scalar_prefetch=2,
            grid=(n_tiles, nc),
            in_specs=[pl.BlockSpec((te, half), xs_map),
                      pl.BlockSpec((None, d, tce), up_map), pl.BlockSpec((None, 1, tce), up_map),
                      pl.BlockSpec((None, d, tce), up_map), pl.BlockSpec((None, 1, tce), up_map),
                      pl.BlockSpec((None, tce, d), down_map), pl.BlockSpec((None, 1, d), bd_map)],
            out_specs=pl.BlockSpec((te, half), lambda t, c, te_ref, nu_ref: (t, 0)),
            scratch_shapes=[pltpu.VMEM((te, d), BF16), pltpu.VMEM((te, d), F32)],
        ),
        out_shape=jax.ShapeDtypeStruct((n_slots, half), U32),
        compiler_params=_params("arbitrary", "arbitrary"),
        name="expert_mlp",
    )(tile_e, n_used, xs, wg, bg.reshape(n_exp, 1, de), wu, bu.reshape(n_exp, 1, de), wd, bd.reshape(n_exp, 1, d))


def _combine_kernel(dest_ref, ys_ref, x_ref, wt_ref, o_ref, gbuf, sem, *, rows):
    base = pl.program_id(0) * rows
    half = gbuf.shape[2]

    def issue(r, c):
        for k in range(TOP_K):
            d = dest_ref[(base + r) * TOP_K + k]
            pltpu.make_async_copy(ys_ref.at[pl.ds(d, 1)], gbuf.at[k, pl.ds(r, 1)], sem).start()
        return c

    lax.fori_loop(0, rows, issue, 0)

    def drain(r, c):
        for k in range(TOP_K):
            pltpu.make_async_copy(ys_ref.at[pl.ds(0, 1)], gbuf.at[k, pl.ds(0, 1)], sem).wait()
        return c

    lax.fori_loop(0, rows, drain, 0)

    lo_sum = x_ref[:, :half]
    hi_sum = x_ref[:, half:]
    for k in range(TOP_K):
        lo, hi = _unpack_halves(gbuf[k])
        wk = wt_ref[:, k:k + 1]
        lo_sum = lo_sum + wk * lo
        hi_sum = hi_sum + wk * hi
    o_ref[:, :half] = lo_sum
    o_ref[:, half:] = hi_sum


def _combine(dest_flat, ys, x1, wt, rows=256):
    n, d = x1.shape
    rows = _tile(n, rows)
    return pl.pallas_call(
        functools.partial(_combine_kernel, rows=rows),
        grid_spec=pltpu.PrefetchScalarGridSpec(
            num_scalar_prefetch=1,
            grid=(n // rows,),
            in_specs=[pl.BlockSpec(memory_space=pl.ANY),
                      pl.BlockSpec((rows, d), lambda i, dest: (i, 0)),
                      pl.BlockSpec((rows, LANES), lambda i, dest: (i, 0))],
            out_specs=pl.BlockSpec((rows, d), lambda i, dest: (i, 0)),
            scratch_shapes=[pltpu.VMEM((TOP_K, rows, d // 2), U32), pltpu.SemaphoreType.DMA(())],
        ),
        out_shape=jax.ShapeDtypeStruct((n, d), F32),
        compiler_params=_params("arbitrary"),
        name="expert_combine",
    )(dest_flat, ys, x1, wt)


def _routed_experts(x1, norm_g, w_router, b_router, wg, bg, wu, bu, wd, bd, te=512):
    n, d = x1.shape
    n_exp = w_router.shape[1]
    hp, idx_p, wt_p, rank_p, cnt = _router(x1, norm_g, w_router, b_router)
    top_idx = idx_p[:, :TOP_K]
    rank = rank_p[:, :TOP_K]

    te = min(te, n * TOP_K)
    counts = cnt[0, :n_exp].astype(I32)
    padded = ((counts + te - 1) // te) * te
    pend = jnp.cumsum(padded)
    pstart = pend - padded
    n_tiles = (n * TOP_K) // te + n_exp
    n_used = (pend[-1] // te).astype(I32).reshape(1)
    tile_e = jnp.clip(jnp.searchsorted(pend, jnp.arange(n_tiles, dtype=I32) * te, side="right"), 0, n_exp - 1).astype(I32)
    dest = (pstart[top_idx] + rank).reshape(n * TOP_K).astype(I32)
    tiles = jnp.arange(n_tiles, dtype=I32)
    partial_last = jnp.any((tiles[:, None] == (pend // te - 1)[None, :]) & ((counts % te) != 0)[None, :], axis=1)
    zero_flag = (partial_last | (tiles >= n_used[0])).astype(I32)

    xs = _dispatch(dest, zero_flag, hp, te)
    ys = _expert_mlp(tile_e, n_used, xs, wg, bg, wu, bu, wd, bd, te)
    return _combine(dest, ys, x1, wt_p)


def kernel(x, mem, positions, norm1_g, w_in, q_lat_g, kv_lat_g, w_uq, w_ukv, qn_nope_g, qn_rope_g, kn_nope_g, kn_rope_g, conv_w, mem_norm_g, w_mem_kv, mem_qn_g, mem_kn_g, w_branch, w_out, norm2_g, w_router, b_router, w_gate, b_gate, w_up, b_up, w_down, b_down):
    b, s, d = x.shape
    n = b * s
    depth = w_in.shape[0]
    q_lora = w_uq.shape[1]
    kv_lora = w_ukv.shape[1]
    heads = w_ukv.shape[2] // (NOPE_DIM + V_DIM)
    mem_len = mem.shape[1]
    assert heads * V_DIM == d and d % (2 * LANES) == 0
    lat = q_lora + kv_lora
    kr_end = lat + ROPE_DIM
    assert w_in.shape[2] == kr_end + 7 * d

    pos = positions.reshape(n, 1).astype(I32)
    xf = x.reshape(n, d)
    mem_f = mem.reshape(b * mem_len, d)

    for l in range(depth):
        w_l = w_in[l]
        w_big = w_l[:, kr_end:].astype(BF16)
        w_small = jnp.concatenate(
            [w_l[:, :kr_end], w_l[:, lat + ROPE_DIM // 2:kr_end], w_l[:, lat:lat + ROPE_DIM // 2]], axis=1).astype(BF16)
        wq3 = w_uq[l].reshape(q_lora, heads, QK_DIM)
        wq = jnp.concatenate([wq3, wq3[..., NOPE_DIM + ROPE_DIM // 2:], wq3[..., NOPE_DIM:NOPE_DIM + ROPE_DIM // 2]],
                             axis=-1).reshape(q_lora, heads * HEAD_COLS).astype(BF16)
        wkv = w_ukv[l].astype(BF16)

        xn = _rmsnorm(xf, norm1_g[l])
        big = _matmul(xn, w_big, BF16, name="in_proj_wide")
        small = _matmul(xn, w_small, BF16, tn=w_small.shape[1], name="in_proj_latent")

        q, k, v = _mla_projections(small, pos, b, s, heads, q_lora, kv_lora, q_lat_g[l], kv_lat_g[l], wq, wkv,
                                   qn_nope_g[l], qn_rope_g[l], kn_nope_g[l], kn_rope_g[l])
        a_mla = _flash_attention(q, k, v)

        a_conv = _short_conv(big, conv_w[l], n, s, d)

        mem_n = _rmsnorm(mem_f, mem_norm_g[l])
        kvm = _matmul(mem_n, w_mem_kv[l].astype(BF16), BF16, name="mem_kv_proj")
        a_mem = _memory_attention(big, kvm, mem_qn_g[l], mem_kn_g[l], b, s, d, mem_len, q_seg=3)

        merged = _gated_merge(a_mla, a_conv, a_mem, w_branch[l].astype(BF16), big, gate_seg=4)
        x1 = _matmul(merged, w_out[l].astype(BF16), F32, res=xf, name="out_proj")

        xf = _routed_experts(x1, norm2_g[l], w_router[l], b_router[l],
                             w_gate[l].astype(BF16), b_gate[l], w_up[l].astype(BF16), b_up[l],
                             w_down[l].astype(BF16), b_down[l])
    return xf.reshape(b, s, d)
```

```python
import functools

import jax
import jax.numpy as jnp
from jax import lax
from jax.experimental import pallas as pl
from jax.experimental.pallas import tpu as pltpu

F32 = jnp.float32
BF16 = jnp.bfloat16
U32 = jnp.uint32
I32 = jnp.int32

EPS = 1e-6
NOPE_DIM = 128
ROPE_DIM = 64
V_DIM = 128
QK_DIM = NOPE_DIM + ROPE_DIM
HEAD_COLS = 256
MEM_HEADS = 4
TOP_K = 4
ROPE_THETA = 10000.0
SWIGLU_LIMIT = 7.0
SWIGLU_ALPHA = 1.702
LANES = 128
V7X_VMEM_LIMIT_BYTES = 56 * 1024 * 1024
HI_MASK = 0xFFFF0000
LOG2_E = 1.4426950408889634


def _params(*sem):
    return pltpu.CompilerParams(dimension_semantics=sem, vmem_limit_bytes=V7X_VMEM_LIMIT_BYTES)


def _tile(n, want):
    t = min(n, want)
    while n % t:
        t //= 2
    assert t >= 1
    return t


def _bits(x):
    return lax.bitcast_convert_type(x, U32)


def _pack_pair(lo, hi):
    lo_bits = _bits(lo.astype(BF16).astype(F32)) >> 16
    hi_bits = _bits(hi.astype(BF16).astype(F32)) & jnp.uint32(HI_MASK)
    return lo_bits | hi_bits


def _pack_halves(y):
    half = y.shape[1] // 2
    return _pack_pair(y[:, :half], y[:, half:])


def _unpack_halves(p):
    lo = lax.bitcast_convert_type(p << 16, F32)
    hi = lax.bitcast_convert_type(p & jnp.uint32(HI_MASK), F32)
    return lo, hi


def _rmsnorm_kernel(x_ref, g_ref, o_ref):
    x = x_ref[...].astype(F32)
    r = lax.rsqrt(jnp.mean(x * x, axis=-1, keepdims=True) + EPS)
    o_ref[...] = (x * r * g_ref[...]).astype(o_ref.dtype)


def _rmsnorm(x, g, tm=512):
    m, d = x.shape
    tm = _tile(m, tm)
    return pl.pallas_call(
        _rmsnorm_kernel,
        grid=(m // tm,),
        in_specs=[pl.BlockSpec((tm, d), lambda i: (i, 0)), pl.BlockSpec((1, d), lambda i: (0, 0))],
        out_specs=pl.BlockSpec((tm, d), lambda i: (i, 0)),
        out_shape=jax.ShapeDtypeStruct((m, d), BF16),
        compiler_params=_params("parallel"),
        name="rmsnorm",
    )(x, g.reshape(1, d))


def _mm_kernel(x_ref, w_ref, o_ref):
    o_ref[...] = jnp.dot(x_ref[...], w_ref[...], preferred_element_type=F32).astype(o_ref.dtype)


def _mm_res_kernel(x_ref, w_ref, r_ref, o_ref):
    o_ref[...] = r_ref[...] + jnp.dot(x_ref[...], w_ref[...], preferred_element_type=F32)


def _matmul(x, w, out_dtype, tm=1024, tn=1024, res=None, name="matmul"):
    m, k = x.shape
    n = w.shape[1]
    tm, tn = _tile(m, tm), _tile(n, tn)
    in_specs = [pl.BlockSpec((tm, k), lambda i, j: (i, 0)), pl.BlockSpec((k, tn), lambda i, j: (0, j))]
    args = [x, w]
    body = _mm_kernel
    if res is not None:
        in_specs.append(pl.BlockSpec((tm, tn), lambda i, j: (i, j)))
        args.append(res)
        body = _mm_res_kernel
    return pl.pallas_call(
        body,
        grid=(m // tm, n // tn),
        in_specs=in_specs,
        out_specs=pl.BlockSpec((tm, tn), lambda i, j: (i, j)),
        out_shape=jax.ShapeDtypeStruct((m, n), out_dtype),
        compiler_params=_params("parallel", "parallel"),
        name=name,
    )(*args)


def _rope_tables(pos_ref, invf_ref, sgn_ref, cos_s, sin_s):
    ang = pos_ref[...].astype(F32) * invf_ref[...]
    cos_s[...] = jnp.cos(ang)
    sin_s[...] = jnp.sin(ang) * sgn_ref[...]


def _rope_pair(slab, gain, cos, sin):
    r = lax.rsqrt(jnp.mean(slab * slab, axis=-1, keepdims=True) + EPS)
    slab = slab * r * gain
    return slab * cos + pltpu.roll(slab, ROPE_DIM, 1) * sin


def _qproj_kernel(cq_ref, pos_ref, invf_ref, sgn_ref, glat_ref, gq_ref, w_ref, o_ref, cqn_s, cos_s, sin_s, *, scale):
    @pl.when(pl.program_id(1) == 0)
    def _():
        c = cq_ref[...].astype(F32)
        r = lax.rsqrt(jnp.mean(c * c, axis=-1, keepdims=True) + EPS)
        cqn_s[...] = (c * r * glat_ref[...]).astype(BF16)
        _rope_tables(pos_ref, invf_ref, sgn_ref, cos_s, sin_s)

    q_all = jnp.dot(cqn_s[...], w_ref[...], preferred_element_type=F32)
    for g in range(o_ref.shape[0]):
        q = q_all[:, g * HEAD_COLS:(g + 1) * HEAD_COLS]
        nope = q[:, :NOPE_DIM]
        rn = lax.rsqrt(jnp.mean(nope * nope, axis=-1, keepdims=True) + EPS)
        nope = nope * rn * gq_ref[:, :NOPE_DIM]
        rot = _rope_pair(q[:, NOPE_DIM:], gq_ref[:, NOPE_DIM:], cos_s[...], sin_s[...])
        o_ref[g, :, :NOPE_DIM] = (nope * scale).astype(o_ref.dtype)
        o_ref[g, :, NOPE_DIM:] = (rot[:, :ROPE_DIM] * scale).astype(o_ref.dtype)


def _kvproj_kernel(ckv_ref, kr_ref, pos_ref, invf_ref, sgn_ref, glat_ref, gkr_ref, gkn_ref, w_ref,
                   k_ref, v_ref, ckvn_s, kr_s, cos_s, sin_s):
    @pl.when(pl.program_id(1) == 0)
    def _():
        c = ckv_ref[...].astype(F32)
        r = lax.rsqrt(jnp.mean(c * c, axis=-1, keepdims=True) + EPS)
        ckvn_s[...] = (c * r * glat_ref[...]).astype(BF16)
        _rope_tables(pos_ref, invf_ref, sgn_ref, cos_s, sin_s)
        kr_s[...] = _rope_pair(kr_ref[...].astype(F32), gkr_ref[...], cos_s[...], sin_s[...]).astype(BF16)

    kv_all = jnp.dot(ckvn_s[...], w_ref[...], preferred_element_type=F32)
    for g in range(k_ref.shape[0]):
        kv = kv_all[:, g * HEAD_COLS:(g + 1) * HEAD_COLS]
        kn = kv[:, :NOPE_DIM]
        rn = lax.rsqrt(jnp.mean(kn * kn, axis=-1, keepdims=True) + EPS)
        k_ref[g, :, :NOPE_DIM] = (kn * rn * gkn_ref[...]).astype(k_ref.dtype)
        k_ref[g, :, NOPE_DIM:] = kr_s[:, :ROPE_DIM]
        v_ref[g] = kv[:, NOPE_DIM:].astype(v_ref.dtype)


def _rope_consts():
    inv_freq = ROPE_THETA ** (-jnp.arange(0, ROPE_DIM, 2, dtype=F32) / ROPE_DIM)
    invf = jnp.tile(inv_freq, LANES // (ROPE_DIM // 2)).reshape(1, LANES)
    half = jnp.concatenate([-jnp.ones((ROPE_DIM // 2,), F32), jnp.ones((ROPE_DIM // 2,), F32)])
    sgn = jnp.tile(half, LANES // ROPE_DIM).reshape(1, LANES)
    return invf, sgn


def _swap_halves(g):
    return jnp.concatenate([g[..., ROPE_DIM // 2:], g[..., :ROPE_DIM // 2]], axis=-1)


def _mla_projections(small, pos, b, s, heads, q_lora, kv_lora, q_lat_g, kv_lat_g, wq, wkv,
                     qn_nope_g, qn_rope_g, kn_nope_g, kn_rope_g, tm=512, hps=8):
    n = b * s
    tm = _tile(s, tm)
    spb = s // tm
    invf, sgn = _rope_consts()
    scale = float(QK_DIM) ** -0.5 * LOG2_E
    gq = jnp.concatenate([qn_nope_g, qn_rope_g, _swap_halves(qn_rope_g)]).reshape(1, HEAD_COLS)
    gkr = jnp.concatenate([kn_rope_g, _swap_halves(kn_rope_g)]).reshape(1, LANES)
    hps = _tile(heads, hps)
    row = lambda i, h: (i, 0)
    const = lambda i, h: (0, 0)
    head_out = lambda i, h: (i // spb, h, i % spb, 0)
    assert q_lora % kv_lora == 0 and kv_lora % LANES == 0
    ckv_blk = q_lora // kv_lora
    kr_blk = (q_lora + kv_lora) // LANES

    q = pl.pallas_call(
        functools.partial(_qproj_kernel, scale=scale),
        grid=(n // tm, heads // hps),
        in_specs=[pl.BlockSpec((tm, q_lora), row), pl.BlockSpec((tm, 1), row),
                  pl.BlockSpec((1, LANES), const), pl.BlockSpec((1, LANES), const),
                  pl.BlockSpec((1, q_lora), const), pl.BlockSpec((1, HEAD_COLS), const),
                  pl.BlockSpec((q_lora, hps * HEAD_COLS), lambda i, h: (0, h))],
        out_specs=pl.BlockSpec((None, hps, tm, QK_DIM), head_out),
        out_shape=jax.ShapeDtypeStruct((b, heads, s, QK_DIM), BF16),
        scratch_shapes=[pltpu.VMEM((tm, q_lora), BF16), pltpu.VMEM((tm, LANES), F32), pltpu.VMEM((tm, LANES), F32)],
        compiler_params=_params("parallel", "arbitrary"),
        name="mla_q_proj",
    )(small, pos, invf, sgn, q_lat_g.reshape(1, q_lora), gq, wq)

    k, v = pl.pallas_call(
        _kvproj_kernel,
        grid=(n // tm, heads // hps),
        in_specs=[pl.BlockSpec((tm, kv_lora), lambda i, h: (i, ckv_blk)),
                  pl.BlockSpec((tm, LANES), lambda i, h: (i, kr_blk)),
                  pl.BlockSpec((tm, 1), row),
                  pl.BlockSpec((1, LANES), const), pl.BlockSpec((1, LANES), const),
                  pl.BlockSpec((1, kv_lora), const), pl.BlockSpec((1, LANES), const),
                  pl.BlockSpec((1, NOPE_DIM), const),
                  pl.BlockSpec((kv_lora, hps * HEAD_COLS), lambda i, h: (0, h))],
        out_specs=[pl.BlockSpec((None, hps, tm, QK_DIM), head_out),
                   pl.BlockSpec((None, hps, tm, V_DIM), head_out)],
        out_shape=[jax.ShapeDtypeStruct((b, heads, s, QK_DIM), BF16),
                   jax.ShapeDtypeStruct((b, heads, s, V_DIM), BF16)],
        scratch_shapes=[pltpu.VMEM((tm, kv_lora), BF16), pltpu.VMEM((tm, LANES), BF16),
                        pltpu.VMEM((tm, LANES), F32), pltpu.VMEM((tm, LANES), F32)],
        compiler_params=_params("parallel", "arbitrary"),
        name="mla_kv_proj",
    )(small, small, pos, invf, sgn, kv_lat_g.reshape(1, kv_lora), gkr, kn_nope_g.reshape(1, NOPE_DIM), wkv)
    return q, k, v


def _flash_kernel(q_ref, k_ref, v_ref, o_ref, s_a, s_b, m_s, l_s, acc_s, *, tq, tk):
    i = pl.program_id(2)

    def scores(j, dst):
        ks = k_ref[pl.ds(pl.multiple_of(j * tk, tk), tk), :]
        dst[...] = lax.dot_general(q_ref[...], ks, (((1,), (1,)), ((), ())), preferred_element_type=F32)

    def soft_pv(src, j, masked):
        s = src[...]
        if masked:
            row = lax.broadcasted_iota(I32, (tq, tk), 0)
            col = lax.broadcasted_iota(I32, (tq, tk), 1)
            s = jnp.where(col <= row, s, -jnp.inf)
        m_prev = m_s[...]
        m_new = jnp.maximum(m_prev, jnp.max(s, axis=-1, keepdims=True))
        alpha = jnp.exp2(m_prev - m_new)
        ps = [jnp.exp2(s[:, c * LANES:(c + 1) * LANES] - m_new) for c in range(tk // LANES)]
        part = ps[0]
        for x in ps[1:]:
            part = part + x
        l_s[...] = alpha * l_s[...] + part
        m_s[...] = m_new
        vs = v_ref[pl.ds(pl.multiple_of(j * tk, tk), tk), :]
        p = jnp.concatenate(ps, axis=1).astype(BF16)
        acc_s[...] = alpha * acc_s[...] + jnp.dot(p, vs, preferred_element_type=F32)

    m_s[...] = jnp.full_like(m_s, -jnp.inf)
    l_s[...] = jnp.zeros_like(l_s)
    acc_s[...] = jnp.zeros_like(acc_s)
    scores(0, s_a)

    def pair(jj, c):
        j = 2 * jj
        scores(j + 1, s_b)
        soft_pv(s_a, j, False)
        scores(j + 2, s_a)
        soft_pv(s_b, j + 1, False)
        return c

    lax.fori_loop(0, i // 2, pair, 0)

    @pl.when(i % 2 == 0)
    def _():
        soft_pv(s_a, i, True)

    @pl.when(i % 2 == 1)
    def _():
        scores(i, s_b)
        soft_pv(s_a, i - 1, False)
        soft_pv(s_b, i, True)

    l = jnp.sum(l_s[...], axis=-1, keepdims=True)
    o_ref[...] = (acc_s[...] / l).astype(o_ref.dtype)


def _flash_attention(q, k, v, tq=512):
    b, heads, s, _ = q.shape
    tq = _tile(s, tq)
    tk = tq
    spb = s // tq
    return pl.pallas_call(
        functools.partial(_flash_kernel, tq=tq, tk=tk),
        grid=(b, heads, spb),
        in_specs=[pl.BlockSpec((None, None, tq, QK_DIM), lambda bi, h, i: (bi, h, i, 0)),
                  pl.BlockSpec((None, None, s, QK_DIM), lambda bi, h, i: (bi, h, 0, 0)),
                  pl.BlockSpec((None, None, s, V_DIM), lambda bi, h, i: (bi, h, 0, 0))],
        out_specs=pl.BlockSpec((tq, V_DIM), lambda bi, h, i: (bi * spb + i, h)),
        out_shape=jax.ShapeDtypeStruct((b * s, heads * V_DIM), BF16),
        scratch_shapes=[pltpu.VMEM((tq, tk), F32), pltpu.VMEM((tq, tk), F32),
                        pltpu.VMEM((tq, LANES), F32), pltpu.VMEM((tq, LANES), F32), pltpu.VMEM((tq, V_DIM), F32)],
        compiler_params=_params("parallel", "parallel", "arbitrary"),
        name="mla_flash_attention",
    )(q, k, v)


def _conv_kernel(cb_ref, cc_ref, ch_ref, pc_ref, ph_ref, w_ref, o_ref, *, tiles_per_seq):
    first = (pl.program_id(0) % tiles_per_seq) == 0
    u = cc_ref[...].astype(F32) * ch_ref[...].astype(F32)
    halo = pc_ref[...].astype(F32) * ph_ref[...].astype(F32)
    halo = jnp.where(first, 0.0, halo)
    cat = jnp.concatenate([halo, u], axis=0)
    n_halo = halo.shape[0]
    u1 = pltpu.roll(cat, 1, 0)[n_halo:]
    u2 = pltpu.roll(cat, 2, 0)[n_halo:]
    y = w_ref[0:1, :] * u2 + w_ref[1:2, :] * u1 + w_ref[2:3, :] * u
    o_ref[...] = (cb_ref[...].astype(F32) * y).astype(o_ref.dtype)


def _short_conv(big, conv_w, n, s, d, ts=512, tc=1024):
    ts, tc = _tile(s, ts), _tile(d, tc)
    halo = 8
    cpb = d // tc
    seg = lambda k: (lambda i, j: (i, k * cpb + j))
    prev = lambda k: (lambda i, j: (jnp.maximum(i * (ts // halo) - 1, 0), k * cpb + j))
    return pl.pallas_call(
        functools.partial(_conv_kernel, tiles_per_seq=s // ts),
        grid=(n // ts, cpb),
        in_specs=[pl.BlockSpec((ts, tc), seg(0)), pl.BlockSpec((ts, tc), seg(1)), pl.BlockSpec((ts, tc), seg(2)),
                  pl.BlockSpec((halo, tc), prev(1)), pl.BlockSpec((halo, tc), prev(2)),
                  pl.BlockSpec((conv_w.shape[0], tc), lambda i, j: (0, j))],
        out_specs=pl.BlockSpec((ts, tc), lambda i, j: (i, j)),
        out_shape=jax.ShapeDtypeStruct((n, d), BF16),
        compiler_params=_params("parallel", "parallel"),
        name="short_conv",
    )(big, big, big, big, big, conv_w)


def _memattn_kernel(q_ref, k_ref, v_ref, gq_ref, gk_ref, o_ref, kn_s, *, scale):
    @pl.when(pl.program_id(2) == 0)
    def _():
        k = k_ref[...].astype(F32)
        r = lax.rsqrt(jnp.mean(k * k, axis=-1, keepdims=True) + EPS)
        kn_s[...] = (k * r * gk_ref[...]).astype(BF16)

    q = q_ref[...].astype(F32)
    r = lax.rsqrt(jnp.mean(q * q, axis=-1, keepdims=True) + EPS)
    qn = (q * (r * scale) * gq_ref[...]).astype(BF16)
    s = lax.dot_general(qn, kn_s[...], (((1,), (1,)), ((), ())), preferred_element_type=F32)
    m = jnp.max(s, axis=-1, keepdims=True)
    p = jnp.exp(s - m)
    l = jnp.sum(p, axis=-1, keepdims=True)
    o = jnp.dot(p.astype(BF16), v_ref[...], preferred_element_type=F32)
    o_ref[...] = (o / l).astype(o_ref.dtype)


def _memory_attention(big, kvm, gq, gk, b, s, d, mem_len, q_seg, ts=512):
    hd = d // MEM_HEADS
    ts = _tile(s, ts)
    spb = s // ts
    return pl.pallas_call(
        functools.partial(_memattn_kernel, scale=float(hd) ** -0.5),
        grid=(b, MEM_HEADS, spb),
        in_specs=[pl.BlockSpec((ts, hd), lambda bi, h, i: (bi * spb + i, q_seg * MEM_HEADS + h)),
                  pl.BlockSpec((mem_len, hd), lambda bi, h, i: (bi, h)),
                  pl.BlockSpec((mem_len, hd), lambda bi, h, i: (bi, MEM_HEADS + h)),
                  pl.BlockSpec((1, hd), lambda bi, h, i: (0, 0)),
                  pl.BlockSpec((1, hd), lambda bi, h, i: (0, 0))],
        out_specs=pl.BlockSpec((ts, hd), lambda bi, h, i: (bi * spb + i, h)),
        out_shape=jax.ShapeDtypeStruct((b * s, d), BF16),
        scratch_shapes=[pltpu.VMEM((mem_len, hd), BF16)],
        compiler_params=_params("parallel", "parallel", "arbitrary"),
        name="memory_attention",
    )(big, kvm, kvm, gq.reshape(1, hd), gk.reshape(1, hd))


def _merge_kernel(a0_ref, a1_ref, a2_ref, w_ref, g_ref, o_ref, acc_s):
    br = pl.program_id(2)
    gate = jax.nn.sigmoid(g_ref[...].astype(F32))

    @pl.when(br == 0)
    def _():
        acc_s[...] = gate * jnp.dot(a0_ref[...], w_ref[...], preferred_element_type=F32)

    @pl.when(br == 1)
    def _():
        acc_s[...] += gate * jnp.dot(a1_ref[...], w_ref[...], preferred_element_type=F32)

    @pl.when(br == 2)
    def _():
        o_ref[...] = (acc_s[...] + gate * jnp.dot(a2_ref[...], w_ref[...], preferred_element_type=F32)).astype(o_ref.dtype)


def _gated_merge(a_mla, a_conv, a_mem, w_branch, big, gate_seg, tm=512, tn=1024):
    n, d = a_mla.shape
    tm, tn = _tile(n, tm), _tile(d, tn)
    cpb = d // tn
    lhs = pl.BlockSpec((tm, d), lambda i, j, br: (i, 0))
    return pl.pallas_call(
        _merge_kernel,
        grid=(n // tm, cpb, 3),
        in_specs=[lhs, lhs, lhs,
                  pl.BlockSpec((None, d, tn), lambda i, j, br: (br, 0, j)),
                  pl.BlockSpec((tm, tn), lambda i, j, br: (i, (gate_seg + br) * cpb + j))],
        out_specs=pl.BlockSpec((tm, tn), lambda i, j, br: (i, j)),
        out_shape=jax.ShapeDtypeStruct((n, d), BF16),
        scratch_shapes=[pltpu.VMEM((tm, tn), F32)],
        compiler_params=_params("parallel", "parallel", "arbitrary"),
        name="gated_merge",
    )(a_mla, a_conv, a_mem, w_branch, big)


def _router_kernel(x_ref, g_ref, wr_ref, br_ref, hp_ref, idx_ref, wt_ref, rank_ref, cnt_ref, carry_s, *, n_exp):
    tm = x_ref.shape[0]

    @pl.when(pl.program_id(0) == 0)
    def _():
        carry_s[...] = jnp.zeros_like(carry_s)

    x = x_ref[...]
    r = lax.rsqrt(jnp.mean(x * x, axis=-1, keepdims=True) + EPS)
    h = x * r * g_ref[...]
    hp_ref[...] = _pack_halves(h)

    logits = jnp.dot(h, wr_ref[...], preferred_element_type=F32, precision=lax.Precision.HIGHEST) + br_ref[...]
    lane = lax.broadcasted_iota(I32, (tm, n_exp), 1)
    out_lane = lax.broadcasted_iota(I32, (tm, LANES), 1)
    work = logits
    vals, hots = [], []
    idx_out = jnp.zeros((tm, LANES), I32)
    for k in range(TOP_K):
        m = jnp.max(work, axis=-1, keepdims=True)
        idx = jnp.min(jnp.where(work == m, lane, n_exp), axis=-1, keepdims=True)
        hot = lane == idx
        vals.append(m)
        hots.append(hot)
        idx_out = jnp.where(out_lane == k, idx, idx_out)
        work = jnp.where(hot, -jnp.inf, work)
    idx_ref[...] = idx_out

    exps = [jnp.exp(v - vals[0]) for v in vals]
    denom = exps[0]
    for e in exps[1:]:
        denom = denom + e
    wt_out = jnp.zeros((tm, LANES), F32)
    for k in range(TOP_K):
        wt_out = jnp.where(out_lane == k, exps[k] / denom, wt_out)
    wt_ref[...] = wt_out

    member = jnp.zeros((tm, n_exp), F32)
    for hot in hots:
        member = member + hot.astype(F32)
    tri = (lax.broadcasted_iota(I32, (tm, tm), 1) < lax.broadcasted_iota(I32, (tm, tm), 0)).astype(BF16)
    prefix = jnp.dot(tri, member.astype(BF16), preferred_element_type=F32) + carry_s[0:1, 0:n_exp]
    rank_out = jnp.zeros((tm, LANES), I32)
    for k in range(TOP_K):
        rk = jnp.sum(jnp.where(hots[k], prefix, 0.0), axis=-1, keepdims=True).astype(I32)
        rank_out = jnp.where(out_lane == k, rk, rank_out)
    rank_ref[...] = rank_out

    carry_s[0:1, 0:n_exp] = carry_s[0:1, 0:n_exp] + jnp.sum(member, axis=0, keepdims=True)
    cnt_ref[...] = carry_s[...]


def _router(x1, g, w_router, b_router, tm=512):
    n, d = x1.shape
    n_exp = w_router.shape[1]
    assert n_exp <= LANES
    tm = _tile(n, tm)
    row = lambda i: (i, 0)
    const = lambda i: (0, 0)
    return pl.pallas_call(
        functools.partial(_router_kernel, n_exp=n_exp),
        grid=(n // tm,),
        in_specs=[pl.BlockSpec((tm, d), row), pl.BlockSpec((1, d), const),
                  pl.BlockSpec((d, n_exp), const), pl.BlockSpec((1, n_exp), const)],
        out_specs=[pl.BlockSpec((tm, d // 2), row), pl.BlockSpec((tm, LANES), row), pl.BlockSpec((tm, LANES), row),
                   pl.BlockSpec((tm, LANES), row), pl.BlockSpec((8, LANES), const)],
        out_shape=[jax.ShapeDtypeStruct((n, d // 2), U32), jax.ShapeDtypeStruct((n, LANES), I32),
                   jax.ShapeDtypeStruct((n, LANES), F32), jax.ShapeDtypeStruct((n, LANES), I32),
                   jax.ShapeDtypeStruct((8, LANES), F32)],
        scratch_shapes=[pltpu.VMEM((8, LANES), F32)],
        compiler_params=_params("arbitrary"),
        name="router",
    )(x1, g.reshape(1, d), w_router, b_router.reshape(1, n_exp))


def _dispatch_kernel(dest_ref, zflag_ref, hp_ref, xs_ref, zeros_s, sem, zsem, *, rows):
    base = pl.program_id(0) * rows
    te = zeros_s.shape[0]

    @pl.when(pl.program_id(0) == 0)
    def _():
        zeros_s[...] = jnp.zeros_like(zeros_s)

        def fill(t):
            return pltpu.make_async_copy(zeros_s, xs_ref.at[pl.ds(pl.multiple_of(t * te, te), te)], zsem)

        def start(t, c):
            @pl.when(zflag_ref[t] != 0)
            def _():
                fill(t).start()
            return c

        def finish(t, c):
            @pl.when(zflag_ref[t] != 0)
            def _():
                fill(t).wait()
            return c

        lax.fori_loop(0, zflag_ref.shape[0], start, 0)
        lax.fori_loop(0, zflag_ref.shape[0], finish, 0)

    def issue(r, c):
        for k in range(TOP_K):
            d = dest_ref[(base + r) * TOP_K + k]
            pltpu.make_async_copy(hp_ref.at[pl.ds(r, 1)], xs_ref.at[pl.ds(d, 1)], sem).start()
        return c

    lax.fori_loop(0, rows, issue, 0)

    def drain(r, c):
        for k in range(TOP_K):
            pltpu.make_async_copy(hp_ref.at[pl.ds(0, 1)], xs_ref.at[pl.ds(0, 1)], sem).wait()
        return c

    lax.fori_loop(0, rows, drain, 0)


def _dispatch(dest_flat, zero_flag, hp, te, rows=256):
    n, half = hp.shape
    rows = _tile(n, rows)
    n_slots = zero_flag.shape[0] * te
    return pl.pallas_call(
        functools.partial(_dispatch_kernel, rows=rows),
        grid_spec=pltpu.PrefetchScalarGridSpec(
            num_scalar_prefetch=2,
            grid=(n // rows,),
            in_specs=[pl.BlockSpec((rows, half), lambda i, dest, zflag: (i, 0))],
            out_specs=pl.BlockSpec(memory_space=pl.ANY),
            scratch_shapes=[pltpu.VMEM((te, half), U32), pltpu.SemaphoreType.DMA(()), pltpu.SemaphoreType.DMA(())],
        ),
        out_shape=jax.ShapeDtypeStruct((n_slots, half), U32),
        compiler_params=_params("arbitrary"),
        name="expert_dispatch",
    )(dest_flat, zero_flag, hp)


def _expert_kernel(te_ref, nu_ref, xs_ref, wg_ref, bg_ref, wu_ref, bu_ref, wdl_ref, wdh_ref, bdl_ref, bdh_ref,
                   o_ref, xb_s, act_s, *, n_up):
    t = pl.program_id(0)
    s = pl.program_id(1)
    used = t < nu_ref[0]
    half = xs_ref.shape[1]
    tcu = wg_ref.shape[1]

    @pl.when(jnp.logical_and(used, s == 0))
    def _():
        lo, hi = _unpack_halves(xs_ref[...])
        xb_s[:, :half] = lo.astype(BF16)
        xb_s[:, half:] = hi.astype(BF16)

    for c in range(n_up):
        @pl.when(jnp.logical_and(used, s == c))
        def _():
            xb = xb_s[...]
            g = jnp.dot(xb, wg_ref[...], preferred_element_type=F32) + bg_ref[...]
            u = jnp.dot(xb, wu_ref[...], preferred_element_type=F32) + bu_ref[...]
            g = jnp.minimum(g, SWIGLU_LIMIT)
            u = jnp.clip(u, -SWIGLU_LIMIT, SWIGLU_LIMIT)
            act_s[:, c * tcu:(c + 1) * tcu] = (g * jax.nn.sigmoid(SWIGLU_ALPHA * g) * (u + 1.0)).astype(BF16)

    @pl.when(jnp.logical_and(used, s >= n_up))
    def _():
        a = act_s[...]
        lo = jnp.dot(a, wdl_ref[...], preferred_element_type=F32) + bdl_ref[...]
        hi = jnp.dot(a, wdh_ref[...], preferred_element_type=F32) + bdh_ref[...]
        o_ref[...] = _pack_pair(lo, hi)

    @pl.when(jnp.logical_and(jnp.logical_not(used), s >= n_up))
    def _():
        o_ref[...] = jnp.zeros_like(o_ref)


def _expert_mlp(tile_e, n_used, xs, wg, bg, wu, bu, wd, bd, te, tcu=512, tnd=1024):
    n_slots, half = xs.shape
    d = 2 * half
    n_exp, _, de = wg.shape
    tcu, tnd = _tile(de, tcu), _tile(half, tnd)
    n_up, n_down = de // tcu, half // tnd
    n_steps = n_up + n_down
    n_tiles = n_slots // te

    def live(t, s, te_ref, nu_ref):
        tt = jnp.minimum(t, nu_ref[0] - 1)
        ss = jnp.where(t < nu_ref[0], s, n_steps - 1)
        return tt, jnp.minimum(ss, n_up - 1), jnp.maximum(ss - n_up, 0), te_ref[tt]

    def xs_map(t, s, te_ref, nu_ref):
        return live(t, s, te_ref, nu_ref)[0], 0

    def up_map(t, s, te_ref, nu_ref):
        _, cu, _, e = live(t, s, te_ref, nu_ref)
        return e, 0, cu

    def down_lo_map(t, s, te_ref, nu_ref):
        _, _, nd, e = live(t, s, te_ref, nu_ref)
        return e, 0, nd

    def down_hi_map(t, s, te_ref, nu_ref):
        _, _, nd, e = live(t, s, te_ref, nu_ref)
        return e, 0, n_down + nd

    def out_map(t, s, te_ref, nu_ref):
        return t, jnp.maximum(s - n_up, 0)

    bd3 = bd.reshape(n_exp, 1, d)
    return pl.pallas_call(
        functools.partial(_expert_kernel, n_up=n_up),
        grid_spec=pltpu.PrefetchScalarGridSpec(
            num_scalar_prefetch=2,
            grid=(n_tiles, n_steps),
            in_specs=[pl.BlockSpec((te, half), xs_map),
                      pl.BlockSpec((None, d, tcu), up_map), pl.BlockSpec((None, 1, tcu), up_map),
                      pl.BlockSpec((None, d, tcu), up_map), pl.BlockSpec((None, 1, tcu), up_map),
                      pl.BlockSpec((None, de, tnd), down_lo_map), pl.BlockSpec((None, de, tnd), down_hi_map),
                      pl.BlockSpec((None, 1, tnd), down_lo_map), pl.BlockSpec((None, 1, tnd), down_hi_map)],
            out_specs=pl.BlockSpec((te, tnd), out_map),
            scratch_shapes=[pltpu.VMEM((te, d), BF16), pltpu.VMEM((te, de), BF16)],
        ),
        out_shape=jax.ShapeDtypeStruct((n_slots, half), U32),
        compiler_params=_params("arbitrary", "arbitrary"),
        name="expert_mlp",
    )(tile_e, n_used, xs, wg, bg.reshape(n_exp, 1, de), wu, bu.reshape(n_exp, 1, de), wd, wd, bd3, bd3)


def _combine_kernel(dest_ref, ys_ref, x_ref, wt_ref, o_ref, gbuf, sem, *, rows):
    base = pl.program_id(0) * rows
    half = gbuf.shape[2]

    def issue(r, c):
        for k in range(TOP_K):
            d = dest_ref[(base + r) * TOP_K + k]
            pltpu.make_async_copy(ys_ref.at[pl.ds(d, 1)], gbuf.at[k, pl.ds(r, 1)], sem).start()
        return c

    lax.fori_loop(0, rows, issue, 0)

    def drain(r, c):
        for k in range(TOP_K):
            pltpu.make_async_copy(ys_ref.at[pl.ds(0, 1)], gbuf.at[k, pl.ds(0, 1)], sem).wait()
        return c

    lax.fori_loop(0, rows, drain, 0)

    lo_sum = x_ref[:, :half]
    hi_sum = x_ref[:, half:]
    for k in range(TOP_K):
        lo, hi = _unpack_halves(gbuf[k])
        wk = wt_ref[:, k:k + 1]
        lo_sum = lo_sum + wk * lo
        hi_sum = hi_sum + wk * hi
    o_ref[:, :half] = lo_sum
    o_ref[:, half:] = hi_sum


def _combine(dest_flat, ys, x1, wt, rows=256):
    n, d = x1.shape
    rows = _tile(n, rows)
    return pl.pallas_call(
        functools.partial(_combine_kernel, rows=rows),
        grid_spec=pltpu.PrefetchScalarGridSpec(
            num_scalar_prefetch=1,
            grid=(n // rows,),
            in_specs=[pl.BlockSpec(memory_space=pl.ANY),
                      pl.BlockSpec((rows, d), lambda i, dest: (i, 0)),
                      pl.BlockSpec((rows, LANES), lambda i, dest: (i, 0))],
            out_specs=pl.BlockSpec((rows, d), lambda i, dest: (i, 0)),
            scratch_shapes=[pltpu.VMEM((TOP_K, rows, d // 2), U32), pltpu.SemaphoreType.DMA(())],
        ),
        out_shape=jax.ShapeDtypeStruct((n, d), F32),
        compiler_params=_params("arbitrary"),
        name="expert_combine",
    )(dest_flat, ys, x1, wt)


def _routed_experts(x1, norm_g, w_router, b_router, wg, bg, wu, bu, wd, bd, te=512):
    n, d = x1.shape
    n_exp = w_router.shape[1]
    hp, idx_p, wt_p, rank_p, cnt = _router(x1, norm_g, w_router, b_router)
    top_idx = idx_p[:, :TOP_K]
    rank = rank_p[:, :TOP_K]

    te = min(te, n * TOP_K)
    counts = cnt[0, :n_exp].astype(I32)
    padded = ((counts + te - 1) // te) * te
    pend = jnp.cumsum(padded)
    pstart = pend - padded
    n_tiles = (n * TOP_K) // te + n_exp
    n_used = (pend[-1] // te).astype(I32).reshape(1)
    tile_e = jnp.clip(jnp.searchsorted(pend, jnp.arange(n_tiles, dtype=I32) * te, side="right"), 0, n_exp - 1).astype(I32)
    dest = (pstart[top_idx] + rank).reshape(n * TOP_K).astype(I32)
    tiles = jnp.arange(n_tiles, dtype=I32)
    partial_last = jnp.any((tiles[:, None] == (pend // te - 1)[None, :]) & ((counts % te) != 0)[None, :], axis=1)
    zero_flag = (partial_last | (tiles >= n_used[0])).astype(I32)

    xs = _dispatch(dest, zero_flag, hp, te)
    ys = _expert_mlp(tile_e, n_used, xs, wg, bg, wu, bu, wd, bd, te)
    return _combine(dest, ys, x1, wt_p)


def kernel(x, mem, positions, norm1_g, w_in, q_lat_g, kv_lat_g, w_uq, w_ukv, qn_nope_g, qn_rope_g, kn_nope_g, kn_rope_g, conv_w, mem_norm_g, w_mem_kv, mem_qn_g, mem_kn_g, w_branch, w_out, norm2_g, w_router, b_router, w_gate, b_gate, w_up, b_up, w_down, b_down):
    b, s, d = x.shape
    n = b * s
    depth = w_in.shape[0]
    q_lora = w_uq.shape[1]
    kv_lora = w_ukv.shape[1]
    heads = w_ukv.shape[2] // (NOPE_DIM + V_DIM)
    mem_len = mem.shape[1]
    assert heads * V_DIM == d and d % (2 * LANES) == 0
    lat = q_lora + kv_lora
    kr_end = lat + ROPE_DIM
    assert w_in.shape[2] == kr_end + 7 * d

    pos = positions.reshape(n, 1).astype(I32)
    xf = x.reshape(n, d)
    mem_f = mem.reshape(b * mem_len, d)

    for l in range(depth):
        w_l = w_in[l]
        w_big = w_l[:, kr_end:].astype(BF16)
        w_small = jnp.concatenate(
            [w_l[:, :kr_end], w_l[:, lat + ROPE_DIM // 2:kr_end], w_l[:, lat:lat + ROPE_DIM // 2]], axis=1).astype(BF16)
        wq3 = w_uq[l].reshape(q_lora, heads, QK_DIM)
        wq = jnp.concatenate([wq3, wq3[..., NOPE_DIM + ROPE_DIM // 2:], wq3[..., NOPE_DIM:NOPE_DIM + ROPE_DIM // 2]],
                             axis=-1).reshape(q_lora, heads * HEAD_COLS).astype(BF16)
        wkv = w_ukv[l].astype(BF16)

        xn = _rmsnorm(xf, norm1_g[l])
        big = _matmul(xn, w_big, BF16, name="in_proj_wide")
        small = _matmul(xn, w_small, BF16, tn=w_small.shape[1], name="in_proj_latent")

        q, k, v = _mla_projections(small, pos, b, s, heads, q_lora, kv_lora, q_lat_g[l], kv_lat_g[l], wq, wkv,
                                   qn_nope_g[l], qn_rope_g[l], kn_nope_g[l], kn_rope_g[l])
        a_mla = _flash_attention(q, k, v)

        a_conv = _short_conv(big, conv_w[l], n, s, d)

        mem_n = _rmsnorm(mem_f, mem_norm_g[l])
        kvm = _matmul(mem_n, w_mem_kv[l].astype(BF16), BF16, name="mem_kv_proj")
        a_mem = _memory_attention(big, kvm, mem_qn_g[l], mem_kn_g[l], b, s, d, mem_len, q_seg=3)

        merged = _gated_merge(a_mla, a_conv, a_mem, w_branch[l].astype(BF16), big, gate_seg=4)
        x1 = _matmul(merged, w_out[l].astype(BF16), F32, res=xf, name="out_proj")

        xf = _routed_experts(x1, norm2_g[l], w_router[l], b_router[l],
                             w_gate[l].astype(BF16), b_gate[l], w_up[l].astype(BF16), b_up[l],
                             w_down[l].astype(BF16), b_down[l])
    return xf.reshape(b, s, d)
```

```python
import functools

import jax
import jax.numpy as jnp
from jax import lax
from jax.experimental import pallas as pl
from jax.experimental.pallas import tpu as pltpu

F32 = jnp.float32
BF16 = jnp.bfloat16
U32 = jnp.uint32
I32 = jnp.int32

EPS = 1e-6
NOPE_DIM = 128
ROPE_DIM = 64
V_DIM = 128
QK_DIM = NOPE_DIM + ROPE_DIM
HEAD_COLS = 256
MEM_HEADS = 4
TOP_K = 4
ROPE_THETA = 10000.0
SWIGLU_LIMIT = 7.0
SWIGLU_ALPHA = 1.702
LANES = 128
V7X_VMEM_LIMIT_BYTES = 56 * 1024 * 1024
HI_MASK = 0xFFFF0000
LOG2_E = 1.4426950408889634


def _params(*sem):
    return pltpu.CompilerParams(dimension_semantics=sem, vmem_limit_bytes=V7X_VMEM_LIMIT_BYTES)


def _tile(n, want):
    t = min(n, want)
    while n % t:
        t //= 2
    assert t >= 1
    return t


def _bits(x):
    return lax.bitcast_convert_type(x, U32)


def _pack_pair(lo, hi):
    lo_bits = _bits(lo.astype(BF16).astype(F32)) >> 16
    hi_bits = _bits(hi.astype(BF16).astype(F32)) & jnp.uint32(HI_MASK)
    return lo_bits | hi_bits


def _pack_halves(y):
    half = y.shape[1] // 2
    return _pack_pair(y[:, :half], y[:, half:])


def _unpack_halves(p):
    lo = lax.bitcast_convert_type(p << 16, F32)
    hi = lax.bitcast_convert_type(p & jnp.uint32(HI_MASK), F32)
    return lo, hi


def _rmsnorm_kernel(x_ref, g_ref, o_ref):
    x = x_ref[...].astype(F32)
    r = lax.rsqrt(jnp.mean(x * x, axis=-1, keepdims=True) + EPS)
    o_ref[...] = (x * r * g_ref[...]).astype(o_ref.dtype)


def _rmsnorm(x, g, tm=512):
    m, d = x.shape
    tm = _tile(m, tm)
    return pl.pallas_call(
        _rmsnorm_kernel,
        grid=(m // tm,),
        in_specs=[pl.BlockSpec((tm, d), lambda i: (i, 0)), pl.BlockSpec((1, d), lambda i: (0, 0))],
        out_specs=pl.BlockSpec((tm, d), lambda i: (i, 0)),
        out_shape=jax.ShapeDtypeStruct((m, d), BF16),
        compiler_params=_params("parallel"),
        name="rmsnorm",
    )(x, g.reshape(1, d))


def _mm_kernel(x_ref, w_ref, o_ref):
    o_ref[...] = jnp.dot(x_ref[...], w_ref[...], preferred_element_type=F32).astype(o_ref.dtype)


def _mm_res_kernel(x_ref, w_ref, r_ref, o_ref):
    o_ref[...] = r_ref[...] + jnp.dot(x_ref[...], w_ref[...], preferred_element_type=F32)


def _matmul(x, w, out_dtype, tm=1024, tn=1024, res=None, name="matmul"):
    m, k = x.shape
    n = w.shape[1]
    tm, tn = _tile(m, tm), _tile(n, tn)
    in_specs = [pl.BlockSpec((tm, k), lambda i, j: (i, 0)), pl.BlockSpec((k, tn), lambda i, j: (0, j))]
    args = [x, w]
    body = _mm_kernel
    if res is not None:
        in_specs.append(pl.BlockSpec((tm, tn), lambda i, j: (i, j)))
        args.append(res)
        body = _mm_res_kernel
    return pl.pallas_call(
        body,
        grid=(m // tm, n // tn),
        in_specs=in_specs,
        out_specs=pl.BlockSpec((tm, tn), lambda i, j: (i, j)),
        out_shape=jax.ShapeDtypeStruct((m, n), out_dtype),
        compiler_params=_params("parallel", "parallel"),
        name=name,
    )(*args)


def _rope_tables(pos_ref, invf_ref, sgn_ref, cos_s, sin_s):
    ang = pos_ref[...].astype(F32) * invf_ref[...]
    cos_s[...] = jnp.cos(ang)
    sin_s[...] = jnp.sin(ang) * sgn_ref[...]


def _rope_pair(slab, gain, cos, sin):
    r = lax.rsqrt(jnp.mean(slab * slab, axis=-1, keepdims=True) + EPS)
    slab = slab * r * gain
    return slab * cos + pltpu.roll(slab, ROPE_DIM, 1) * sin


def _qproj_kernel(cq_ref, pos_ref, invf_ref, sgn_ref, glat_ref, gq_ref, w_ref, o_ref, cqn_s, cos_s, sin_s, *, scale):
    @pl.when(pl.program_id(1) == 0)
    def _():
        c = cq_ref[...].astype(F32)
        r = lax.rsqrt(jnp.mean(c * c, axis=-1, keepdims=True) + EPS)
        cqn_s[...] = (c * r * glat_ref[...]).astype(BF16)
        _rope_tables(pos_ref, invf_ref, sgn_ref, cos_s, sin_s)

    q_all = jnp.dot(cqn_s[...], w_ref[...], preferred_element_type=F32)
    for g in range(o_ref.shape[0]):
        q = q_all[:, g * HEAD_COLS:(g + 1) * HEAD_COLS]
        nope = q[:, :NOPE_DIM]
        rn = lax.rsqrt(jnp.mean(nope * nope, axis=-1, keepdims=True) + EPS)
        nope = nope * rn * gq_ref[:, :NOPE_DIM]
        rot = _rope_pair(q[:, NOPE_DIM:], gq_ref[:, NOPE_DIM:], cos_s[...], sin_s[...])
        o_ref[g, :, :NOPE_DIM] = (nope * scale).astype(o_ref.dtype)
        o_ref[g, :, NOPE_DIM:] = (rot[:, :ROPE_DIM] * scale).astype(o_ref.dtype)


def _kvproj_kernel(ckv_ref, kr_ref, pos_ref, invf_ref, sgn_ref, glat_ref, gkr_ref, gkn_ref, w_ref,
                   k_ref, v_ref, ckvn_s, kr_s, cos_s, sin_s):
    @pl.when(pl.program_id(1) == 0)
    def _():
        c = ckv_ref[...].astype(F32)
        r = lax.rsqrt(jnp.mean(c * c, axis=-1, keepdims=True) + EPS)
        ckvn_s[...] = (c * r * glat_ref[...]).astype(BF16)
        _rope_tables(pos_ref, invf_ref, sgn_ref, cos_s, sin_s)
        kr_s[...] = _rope_pair(kr_ref[...].astype(F32), gkr_ref[...], cos_s[...], sin_s[...]).astype(BF16)

    kv_all = jnp.dot(ckvn_s[...], w_ref[...], preferred_element_type=F32)
    for g in range(k_ref.shape[0]):
        kv = kv_all[:, g * HEAD_COLS:(g + 1) * HEAD_COLS]
        kn = kv[:, :NOPE_DIM]
        rn = lax.rsqrt(jnp.mean(kn * kn, axis=-1, keepdims=True) + EPS)
        k_ref[g, :, :NOPE_DIM] = (kn * rn * gkn_ref[...]).astype(k_ref.dtype)
        k_ref[g, :, NOPE_DIM:] = kr_s[:, :ROPE_DIM]
        v_ref[g] = kv[:, NOPE_DIM:].astype(v_ref.dtype)


def _rope_consts():
    inv_freq = ROPE_THETA ** (-jnp.arange(0, ROPE_DIM, 2, dtype=F32) / ROPE_DIM)
    invf = jnp.tile(inv_freq, LANES // (ROPE_DIM // 2)).reshape(1, LANES)
    half = jnp.concatenate([-jnp.ones((ROPE_DIM // 2,), F32), jnp.ones((ROPE_DIM // 2,), F32)])
    sgn = jnp.tile(half, LANES // ROPE_DIM).reshape(1, LANES)
    return invf, sgn


def _swap_halves(g):
    return jnp.concatenate([g[..., ROPE_DIM // 2:], g[..., :ROPE_DIM // 2]], axis=-1)


def _mla_projections(small, pos, b, s, heads, q_lora, kv_lora, q_lat_g, kv_lat_g, wq, wkv,
                     qn_nope_g, qn_rope_g, kn_nope_g, kn_rope_g, tm=512, hps=8):
    n = b * s
    tm = _tile(s, tm)
    spb = s // tm
    invf, sgn = _rope_consts()
    scale = float(QK_DIM) ** -0.5 * LOG2_E
    gq = jnp.concatenate([qn_nope_g, qn_rope_g, _swap_halves(qn_rope_g)]).reshape(1, HEAD_COLS)
    gkr = jnp.concatenate([kn_rope_g, _swap_halves(kn_rope_g)]).reshape(1, LANES)
    hps = _tile(heads, hps)
    row = lambda i, h: (i, 0)
    const = lambda i, h: (0, 0)
    head_out = lambda i, h: (i // spb, h, i % spb, 0)
    assert q_lora % kv_lora == 0 and kv_lora % LANES == 0
    ckv_blk = q_lora // kv_lora
    kr_blk = (q_lora + kv_lora) // LANES

    q = pl.pallas_call(
        functools.partial(_qproj_kernel, scale=scale),
        grid=(n // tm, heads // hps),
        in_specs=[pl.BlockSpec((tm, q_lora), row), pl.BlockSpec((tm, 1), row),
                  pl.BlockSpec((1, LANES), const), pl.BlockSpec((1, LANES), const),
                  pl.BlockSpec((1, q_lora), const), pl.BlockSpec((1, HEAD_COLS), const),
                  pl.BlockSpec((q_lora, hps * HEAD_COLS), lambda i, h: (0, h))],
        out_specs=pl.BlockSpec((None, hps, tm, QK_DIM), head_out),
        out_shape=jax.ShapeDtypeStruct((b, heads, s, QK_DIM), BF16),
        scratch_shapes=[pltpu.VMEM((tm, q_lora), BF16), pltpu.VMEM((tm, LANES), F32), pltpu.VMEM((tm, LANES), F32)],
        compiler_params=_params("parallel", "arbitrary"),
        name="mla_q_proj",
    )(small, pos, invf, sgn, q_lat_g.reshape(1, q_lora), gq, wq)

    k, v = pl.pallas_call(
        _kvproj_kernel,
        grid=(n // tm, heads // hps),
        in_specs=[pl.BlockSpec((tm, kv_lora), lambda i, h: (i, ckv_blk)),
                  pl.BlockSpec((tm, LANES), lambda i, h: (i, kr_blk)),
                  pl.BlockSpec((tm, 1), row),
                  pl.BlockSpec((1, LANES), const), pl.BlockSpec((1, LANES), const),
                  pl.BlockSpec((1, kv_lora), const), pl.BlockSpec((1, LANES), const),
                  pl.BlockSpec((1, NOPE_DIM), const),
                  pl.BlockSpec((kv_lora, hps * HEAD_COLS), lambda i, h: (0, h))],
        out_specs=[pl.BlockSpec((None, hps, tm, QK_DIM), head_out),
                   pl.BlockSpec((None, hps, tm, V_DIM), head_out)],
        out_shape=[jax.ShapeDtypeStruct((b, heads, s, QK_DIM), BF16),
                   jax.ShapeDtypeStruct((b, heads, s, V_DIM), BF16)],
        scratch_shapes=[pltpu.VMEM((tm, kv_lora), BF16), pltpu.VMEM((tm, LANES), BF16),
                        pltpu.VMEM((tm, LANES), F32), pltpu.VMEM((tm, LANES), F32)],
        compiler_params=_params("parallel", "arbitrary"),
        name="mla_kv_proj",
    )(small, small, pos, invf, sgn, kv_lat_g.reshape(1, kv_lora), gkr, kn_nope_g.reshape(1, NOPE_DIM), wkv)
    return q, k, v


def _flash_kernel(q_ref, k_ref, v_ref, o_ref, s_a, s_b, m_s, l_s, acc_s, *, tq, tk):
    i = pl.program_id(2)

    def scores(j, dst, r0=0):
        ks = k_ref[pl.ds(pl.multiple_of(j * tk, tk), tk), :]
        dst[r0:, :] = lax.dot_general(q_ref[r0:, :], ks, (((1,), (1,)), ((), ())), preferred_element_type=F32)

    def soft_pv(src, j, r0=0, diagonal=False):
        s = src[r0:, :]
        if diagonal:
            row = lax.broadcasted_iota(I32, s.shape, 0)
            col = lax.broadcasted_iota(I32, s.shape, 1)
            s = jnp.where(col <= row, s, -jnp.inf)
        m_prev = m_s[r0:, :]
        m_new = jnp.maximum(m_prev, jnp.max(s, axis=-1, keepdims=True))
        alpha = jnp.exp2(m_prev - m_new)
        ps = [jnp.exp2(s[:, c * LANES:(c + 1) * LANES] - m_new) for c in range(tk // LANES)]
        part = ps[0]
        for x in ps[1:]:
            part = part + x
        l_s[r0:, :] = alpha * l_s[r0:, :] + part
        m_s[r0:, :] = m_new
        vs = v_ref[pl.ds(pl.multiple_of(j * tk, tk), tk), :]
        p = jnp.concatenate(ps, axis=1).astype(BF16)
        acc_s[r0:, :] = alpha * acc_s[r0:, :] + jnp.dot(p, vs, preferred_element_type=F32)

    m_s[...] = jnp.full_like(m_s, -jnp.inf)
    l_s[...] = jnp.zeros_like(l_s)
    acc_s[...] = jnp.zeros_like(acc_s)
    scores(0, s_a)

    def pair(jj, c):
        j = 2 * jj
        scores(j + 1, s_b)
        soft_pv(s_a, j)
        scores(j + 2, s_a)
        soft_pv(s_b, j + 1)
        return c

    lax.fori_loop(0, i, pair, 0)
    scores(2 * i + 1, s_b, tk)
    soft_pv(s_a, 2 * i, 0, True)
    soft_pv(s_b, 2 * i + 1, tk, True)

    l = jnp.sum(l_s[...], axis=-1, keepdims=True)
    o_ref[...] = (acc_s[...] / l).astype(o_ref.dtype)


def _flash_attention(q, k, v, tq=1024):
    b, heads, s, _ = q.shape
    tq = _tile(s, tq)
    tk = tq // 2
    assert tk % LANES == 0
    spb = s // tq
    return pl.pallas_call(
        functools.partial(_flash_kernel, tq=tq, tk=tk),
        grid=(b, heads, spb),
        in_specs=[pl.BlockSpec((None, None, tq, QK_DIM), lambda bi, h, i: (bi, h, i, 0)),
                  pl.BlockSpec((None, None, s, QK_DIM), lambda bi, h, i: (bi, h, 0, 0)),
                  pl.BlockSpec((None, None, s, V_DIM), lambda bi, h, i: (bi, h, 0, 0))],
        out_specs=pl.BlockSpec((tq, V_DIM), lambda bi, h, i: (bi * spb + i, h)),
        out_shape=jax.ShapeDtypeStruct((b * s, heads * V_DIM), BF16),
        scratch_shapes=[pltpu.VMEM((tq, tk), F32), pltpu.VMEM((tq, tk), F32),
                        pltpu.VMEM((tq, LANES), F32), pltpu.VMEM((tq, LANES), F32), pltpu.VMEM((tq, V_DIM), F32)],
        compiler_params=_params("parallel", "parallel", "arbitrary"),
        name="mla_flash_attention",
    )(q, k, v)


def _conv_kernel(cb_ref, cc_ref, ch_ref, pc_ref, ph_ref, w_ref, o_ref, *, tiles_per_seq):
    first = (pl.program_id(0) % tiles_per_seq) == 0
    u = cc_ref[...].astype(F32) * ch_ref[...].astype(F32)
    halo = pc_ref[...].astype(F32) * ph_ref[...].astype(F32)
    halo = jnp.where(first, 0.0, halo)
    cat = jnp.concatenate([halo, u], axis=0)
    n_halo = halo.shape[0]
    u1 = pltpu.roll(cat, 1, 0)[n_halo:]
    u2 = pltpu.roll(cat, 2, 0)[n_halo:]
    y = w_ref[0:1, :] * u2 + w_ref[1:2, :] * u1 + w_ref[2:3, :] * u
    o_ref[...] = (cb_ref[...].astype(F32) * y).astype(o_ref.dtype)


def _short_conv(big, conv_w, n, s, d, ts=512, tc=1024):
    ts, tc = _tile(s, ts), _tile(d, tc)
    halo = 8
    cpb = d // tc
    seg = lambda k: (lambda i, j: (i, k * cpb + j))
    prev = lambda k: (lambda i, j: (jnp.maximum(i * (ts // halo) - 1, 0), k * cpb + j))
    return pl.pallas_call(
        functools.partial(_conv_kernel, tiles_per_seq=s // ts),
        grid=(n // ts, cpb),
        in_specs=[pl.BlockSpec((ts, tc), seg(0)), pl.BlockSpec((ts, tc), seg(1)), pl.BlockSpec((ts, tc), seg(2)),
                  pl.BlockSpec((halo, tc), prev(1)), pl.BlockSpec((halo, tc), prev(2)),
                  pl.BlockSpec((conv_w.shape[0], tc), lambda i, j: (0, j))],
        out_specs=pl.BlockSpec((ts, tc), lambda i, j: (i, j)),
        out_shape=jax.ShapeDtypeStruct((n, d), BF16),
        compiler_params=_params("parallel", "parallel"),
        name="short_conv",
    )(big, big, big, big, big, conv_w)


def _memattn_kernel(q_ref, k_ref, v_ref, gq_ref, gk_ref, o_ref, kn_s, *, scale):
    @pl.when(pl.program_id(2) == 0)
    def _():
        k = k_ref[...].astype(F32)
        r = lax.rsqrt(jnp.mean(k * k, axis=-1, keepdims=True) + EPS)
        kn_s[...] = (k * r * gk_ref[...]).astype(BF16)

    q = q_ref[...].astype(F32)
    r = lax.rsqrt(jnp.mean(q * q, axis=-1, keepdims=True) + EPS)
    qn = (q * (r * scale) * gq_ref[...]).astype(BF16)
    s = lax.dot_general(qn, kn_s[...], (((1,), (1,)), ((), ())), preferred_element_type=F32)
    m = jnp.max(s, axis=-1, keepdims=True)
    p = jnp.exp(s - m)
    l = jnp.sum(p, axis=-1, keepdims=True)
    o = jnp.dot(p.astype(BF16), v_ref[...], preferred_element_type=F32)
    o_ref[...] = (o / l).astype(o_ref.dtype)


def _memory_attention(big, kvm, gq, gk, b, s, d, mem_len, q_seg, ts=512):
    hd = d // MEM_HEADS
    ts = _tile(s, ts)
    spb = s // ts
    return pl.pallas_call(
        functools.partial(_memattn_kernel, scale=float(hd) ** -0.5),
        grid=(b, MEM_HEADS, spb),
        in_specs=[pl.BlockSpec((ts, hd), lambda bi, h, i: (bi * spb + i, q_seg * MEM_HEADS + h)),
                  pl.BlockSpec((mem_len, hd), lambda bi, h, i: (bi, h)),
                  pl.BlockSpec((mem_len, hd), lambda bi, h, i: (bi, MEM_HEADS + h)),
                  pl.BlockSpec((1, hd), lambda bi, h, i: (0, 0)),
                  pl.BlockSpec((1, hd), lambda bi, h, i: (0, 0))],
        out_specs=pl.BlockSpec((ts, hd), lambda bi, h, i: (bi * spb + i, h)),
        out_shape=jax.ShapeDtypeStruct((b * s, d), BF16),
        scratch_shapes=[pltpu.VMEM((mem_len, hd), BF16)],
        compiler_params=_params("parallel", "parallel", "arbitrary"),
        name="memory_attention",
    )(big, kvm, kvm, gq.reshape(1, hd), gk.reshape(1, hd))


def _merge_kernel(a0_ref, a1_ref, a2_ref, w_ref, g_ref, o_ref, acc_s):
    br = pl.program_id(2)
    gate = jax.nn.sigmoid(g_ref[...].astype(F32))

    @pl.when(br == 0)
    def _():
        acc_s[...] = gate * jnp.dot(a0_ref[...], w_ref[...], preferred_element_type=F32)

    @pl.when(br == 1)
    def _():
        acc_s[...] += gate * jnp.dot(a1_ref[...], w_ref[...], preferred_element_type=F32)

    @pl.when(br == 2)
    def _():
        o_ref[...] = (acc_s[...] + gate * jnp.dot(a2_ref[...], w_ref[...], preferred_element_type=F32)).astype(o_ref.dtype)


def _gated_merge(a_mla, a_conv, a_mem, w_branch, big, gate_seg, tm=512, tn=1024):
    n, d = a_mla.shape
    tm, tn = _tile(n, tm), _tile(d, tn)
    cpb = d // tn
    lhs = pl.BlockSpec((tm, d), lambda i, j, br: (i, 0))
    return pl.pallas_call(
        _merge_kernel,
        grid=(n // tm, cpb, 3),
        in_specs=[lhs, lhs, lhs,
                  pl.BlockSpec((None, d, tn), lambda i, j, br: (br, 0, j)),
                  pl.BlockSpec((tm, tn), lambda i, j, br: (i, (gate_seg + br) * cpb + j))],
        out_specs=pl.BlockSpec((tm, tn), lambda i, j, br: (i, j)),
        out_shape=jax.ShapeDtypeStruct((n, d), BF16),
        scratch_shapes=[pltpu.VMEM((tm, tn), F32)],
        compiler_params=_params("parallel", "parallel", "arbitrary"),
        name="gated_merge",
    )(a_mla, a_conv, a_mem, w_branch, big)


def _router_kernel(x_ref, g_ref, wr_ref, br_ref, hp_ref, idx_ref, wt_ref, rank_ref, cnt_ref, carry_s, *, n_exp):
    tm = x_ref.shape[0]

    @pl.when(pl.program_id(0) == 0)
    def _():
        carry_s[...] = jnp.zeros_like(carry_s)

    x = x_ref[...]
    r = lax.rsqrt(jnp.mean(x * x, axis=-1, keepdims=True) + EPS)
    h = x * r * g_ref[...]
    hp_ref[...] = _pack_halves(h)

    logits = jnp.dot(h, wr_ref[...], preferred_element_type=F32, precision=lax.Precision.HIGHEST) + br_ref[...]
    lane = lax.broadcasted_iota(I32, (tm, n_exp), 1)
    out_lane = lax.broadcasted_iota(I32, (tm, LANES), 1)
    work = logits
    vals, hots = [], []
    idx_out = jnp.zeros((tm, LANES), I32)
    for k in range(TOP_K):
        m = jnp.max(work, axis=-1, keepdims=True)
        idx = jnp.min(jnp.where(work == m, lane, n_exp), axis=-1, keepdims=True)
        hot = lane == idx
        vals.append(m)
        hots.append(hot)
        idx_out = jnp.where(out_lane == k, idx, idx_out)
        work = jnp.where(hot, -jnp.inf, work)
    idx_ref[...] = idx_out

    exps = [jnp.exp(v - vals[0]) for v in vals]
    denom = exps[0]
    for e in exps[1:]:
        denom = denom + e
    wt_out = jnp.zeros((tm, LANES), F32)
    for k in range(TOP_K):
        wt_out = jnp.where(out_lane == k, exps[k] / denom, wt_out)
    wt_ref[...] = wt_out

    member = jnp.zeros((tm, n_exp), F32)
    for hot in hots:
        member = member + hot.astype(F32)
    tri = (lax.broadcasted_iota(I32, (tm, tm), 1) < lax.broadcasted_iota(I32, (tm, tm), 0)).astype(BF16)
    prefix = jnp.dot(tri, member.astype(BF16), preferred_element_type=F32) + carry_s[0:1, 0:n_exp]
    rank_out = jnp.zeros((tm, LANES), I32)
    for k in range(TOP_K):
        rk = jnp.sum(jnp.where(hots[k], prefix, 0.0), axis=-1, keepdims=True).astype(I32)
        rank_out = jnp.where(out_lane == k, rk, rank_out)
    rank_ref[...] = rank_out

    carry_s[0:1, 0:n_exp] = carry_s[0:1, 0:n_exp] + jnp.sum(member, axis=0, keepdims=True)
    cnt_ref[...] = carry_s[...]


def _router(x1, g, w_router, b_router, tm=512):
    n, d = x1.shape
    n_exp = w_router.shape[1]
    assert n_exp <= LANES
    tm = _tile(n, tm)
    row = lambda i: (i, 0)
    const = lambda i: (0, 0)
    return pl.pallas_call(
        functools.partial(_router_kernel, n_exp=n_exp),
        grid=(n // tm,),
        in_specs=[pl.BlockSpec((tm, d), row), pl.BlockSpec((1, d), const),
                  pl.BlockSpec((d, n_exp), const), pl.BlockSpec((1, n_exp), const)],
        out_specs=[pl.BlockSpec((tm, d // 2), row), pl.BlockSpec((tm, LANES), row), pl.BlockSpec((tm, LANES), row),
                   pl.BlockSpec((tm, LANES), row), pl.BlockSpec((8, LANES), const)],
        out_shape=[jax.ShapeDtypeStruct((n, d // 2), U32), jax.ShapeDtypeStruct((n, LANES), I32),
                   jax.ShapeDtypeStruct((n, LANES), F32), jax.ShapeDtypeStruct((n, LANES), I32),
                   jax.ShapeDtypeStruct((8, LANES), F32)],
        scratch_shapes=[pltpu.VMEM((8, LANES), F32)],
        compiler_params=_params("arbitrary"),
        name="router",
    )(x1, g.reshape(1, d), w_router, b_router.reshape(1, n_exp))


def _dispatch_kernel(dest_ref, zflag_ref, hp_ref, xs_ref, zeros_s, sem, zsem, *, rows):
    base = pl.program_id(0) * rows
    te = zeros_s.shape[0]

    @pl.when(pl.program_id(0) == 0)
    def _():
        zeros_s[...] = jnp.zeros_like(zeros_s)

        def fill(t):
            return pltpu.make_async_copy(zeros_s, xs_ref.at[pl.ds(pl.multiple_of(t * te, te), te)], zsem)

        def start(t, c):
            @pl.when(zflag_ref[t] != 0)
            def _():
                fill(t).start()
            return c

        def finish(t, c):
            @pl.when(zflag_ref[t] != 0)
            def _():
                fill(t).wait()
            return c

        lax.fori_loop(0, zflag_ref.shape[0], start, 0)
        lax.fori_loop(0, zflag_ref.shape[0], finish, 0)

    def issue(r, c):
        for k in range(TOP_K):
            d = dest_ref[(base + r) * TOP_K + k]
            pltpu.make_async_copy(hp_ref.at[pl.ds(r, 1)], xs_ref.at[pl.ds(d, 1)], sem).start()
        return c

    lax.fori_loop(0, rows, issue, 0)

    def drain(r, c):
        for k in range(TOP_K):
            pltpu.make_async_copy(hp_ref.at[pl.ds(0, 1)], xs_ref.at[pl.ds(0, 1)], sem).wait()
        return c

    lax.fori_loop(0, rows, drain, 0)


def _dispatch(dest_flat, zero_flag, hp, te, rows=256):
    n, half = hp.shape
    rows = _tile(n, rows)
    n_slots = zero_flag.shape[0] * te
    return pl.pallas_call(
        functools.partial(_dispatch_kernel, rows=rows),
        grid_spec=pltpu.PrefetchScalarGridSpec(
            num_scalar_prefetch=2,
            grid=(n // rows,),
            in_specs=[pl.BlockSpec((rows, half), lambda i, dest, zflag: (i, 0))],
            out_specs=pl.BlockSpec(memory_space=pl.ANY),
            scratch_shapes=[pltpu.VMEM((te, half), U32), pltpu.SemaphoreType.DMA(()), pltpu.SemaphoreType.DMA(())],
        ),
        out_shape=jax.ShapeDtypeStruct((n_slots, half), U32),
        compiler_params=_params("arbitrary"),
        name="expert_dispatch",
    )(dest_flat, zero_flag, hp)


def _group_items(tile_e, first_tile, tiles_of, n_used, n_blocks, n_tiles):
    n_items = n_used * n_blocks
    w = jnp.arange(n_tiles * n_blocks, dtype=I32)
    wc = jnp.minimum(w, n_items - 1)
    e = tile_e[wc // n_blocks]
    first, count = first_tile[e], jnp.maximum(tiles_of[e], 1)
    r = wc - first * n_blocks
    in_tile, in_blk = first + r % count, r // count
    z = jnp.maximum(w - n_items, 0)
    live = w < n_items
    out_tile = jnp.where(live, in_tile, n_used + z // n_blocks)
    out_blk = jnp.where(live, in_blk, z % n_blocks)
    starts = (live & (r % count == 0)).astype(I32)
    return in_tile, in_blk, e, starts, out_tile, out_blk, n_items.reshape(1)


def _expert_up_kernel(tile_ref, blk_ref, exp_ref, start_ref, ot_ref, ob_ref, ni_ref,
                      xs_ref, wg_ref, bg_ref, wu_ref, bu_ref, act_ref, xb_s, wg_s, wu_s):
    w = pl.program_id(0)
    live = w < ni_ref[0]
    half = xs_ref.shape[1]

    @pl.when(start_ref[w] != 0)
    def _():
        wg_s[...] = wg_ref[...].astype(BF16)
        wu_s[...] = wu_ref[...].astype(BF16)

    @pl.when(live)
    def _():
        lo, hi = _unpack_halves(xs_ref[...])
        xb_s[:, :half] = lo.astype(BF16)
        xb_s[:, half:] = hi.astype(BF16)
        xb = xb_s[...]
        g = jnp.dot(xb, wg_s[...], preferred_element_type=F32) + bg_ref[...]
        u = jnp.dot(xb, wu_s[...], preferred_element_type=F32) + bu_ref[...]
        g = jnp.minimum(g, SWIGLU_LIMIT)
        u = jnp.clip(u, -SWIGLU_LIMIT, SWIGLU_LIMIT)
        act_ref[...] = (g * jax.nn.sigmoid(SWIGLU_ALPHA * g) * (u + 1.0)).astype(act_ref.dtype)

    @pl.when(jnp.logical_not(live))
    def _():
        act_ref[...] = jnp.zeros_like(act_ref)


def _expert_down_kernel(tile_ref, blk_ref, exp_ref, start_ref, ot_ref, ob_ref, ni_ref,
                        act_ref, wdl_ref, wdh_ref, bdl_ref, bdh_ref, o_ref, wl_s, wh_s):
    w = pl.program_id(0)
    live = w < ni_ref[0]

    @pl.when(start_ref[w] != 0)
    def _():
        wl_s[...] = wdl_ref[...].astype(BF16)
        wh_s[...] = wdh_ref[...].astype(BF16)

    @pl.when(live)
    def _():
        a = act_ref[...]
        lo = jnp.dot(a, wl_s[...], preferred_element_type=F32) + bdl_ref[...]
        hi = jnp.dot(a, wh_s[...], preferred_element_type=F32) + bdh_ref[...]
        o_ref[...] = _pack_pair(lo, hi)

    @pl.when(jnp.logical_not(live))
    def _():
        o_ref[...] = jnp.zeros_like(o_ref)


def _expert_mlp(tile_e, first_tile, tiles_of, n_used, xs, wg, bg, wu, bu, wd, bd, te, tcu=256, tnd=1024):
    n_slots, half = xs.shape
    d = 2 * half
    n_exp, _, de = wg.shape
    tcu, tnd = _tile(de, tcu), _tile(half, tnd)
    n_up, n_down = de // tcu, half // tnd
    n_tiles = n_slots // te
    n_pre = 7

    in_row = lambda w, tl, bk, ex, st, ot, ob, ni: (tl[w], 0)
    w_blk = lambda w, tl, bk, ex, st, ot, ob, ni: (ex[w], 0, bk[w])
    out_blk = lambda w, tl, bk, ex, st, ot, ob, ni: (ot[w], ob[w])

    up_items = _group_items(tile_e, first_tile, tiles_of, n_used, n_up, n_tiles)
    act = pl.pallas_call(
        _expert_up_kernel,
        grid_spec=pltpu.PrefetchScalarGridSpec(
            num_scalar_prefetch=n_pre,
            grid=(n_tiles * n_up,),
            in_specs=[pl.BlockSpec((te, half), in_row),
                      pl.BlockSpec((None, d, tcu), w_blk), pl.BlockSpec((None, 1, tcu), w_blk),
                      pl.BlockSpec((None, d, tcu), w_blk), pl.BlockSpec((None, 1, tcu), w_blk)],
            out_specs=pl.BlockSpec((te, tcu), out_blk),
            scratch_shapes=[pltpu.VMEM((te, d), BF16), pltpu.VMEM((d, tcu), BF16), pltpu.VMEM((d, tcu), BF16)],
        ),
        out_shape=jax.ShapeDtypeStruct((n_slots, de), BF16),
        compiler_params=_params("arbitrary"),
        name="expert_gate_up",
    )(*up_items, xs, wg, bg.reshape(n_exp, 1, de), wu, bu.reshape(n_exp, 1, de))

    hi_blk = lambda w, tl, bk, ex, st, ot, ob, ni: (ex[w], 0, n_down + bk[w])
    bd3 = bd.reshape(n_exp, 1, d)
    down_items = _group_items(tile_e, first_tile, tiles_of, n_used, n_down, n_tiles)
    return pl.pallas_call(
        _expert_down_kernel,
        grid_spec=pltpu.PrefetchScalarGridSpec(
            num_scalar_prefetch=n_pre,
            grid=(n_tiles * n_down,),
            in_specs=[pl.BlockSpec((te, de), in_row),
                      pl.BlockSpec((None, de, tnd), w_blk), pl.BlockSpec((None, de, tnd), hi_blk),
                      pl.BlockSpec((None, 1, tnd), w_blk), pl.BlockSpec((None, 1, tnd), hi_blk)],
            out_specs=pl.BlockSpec((te, tnd), out_blk),
            scratch_shapes=[pltpu.VMEM((de, tnd), BF16), pltpu.VMEM((de, tnd), BF16)],
        ),
        out_shape=jax.ShapeDtypeStruct((n_slots, half), U32),
        compiler_params=_params("arbitrary"),
        name="expert_down",
    )(*down_items, act, wd, wd, bd3, bd3)


def _combine_kernel(dest_ref, ys_ref, x_ref, wt_ref, o_ref, gbuf, sem, *, rows):
    base = pl.program_id(0) * rows
    half = gbuf.shape[2]

    def issue(r, c):
        for k in range(TOP_K):
            d = dest_ref[(base + r) * TOP_K + k]
            pltpu.make_async_copy(ys_ref.at[pl.ds(d, 1)], gbuf.at[k, pl.ds(r, 1)], sem).start()
        return c

    lax.fori_loop(0, rows, issue, 0)

    def drain(r, c):
        for k in range(TOP_K):
            pltpu.make_async_copy(ys_ref.at[pl.ds(0, 1)], gbuf.at[k, pl.ds(0, 1)], sem).wait()
        return c

    lax.fori_loop(0, rows, drain, 0)

    lo_sum = x_ref[:, :half]
    hi_sum = x_ref[:, half:]
    for k in range(TOP_K):
        lo, hi = _unpack_halves(gbuf[k])
        wk = wt_ref[:, k:k + 1]
        lo_sum = lo_sum + wk * lo
        hi_sum = hi_sum + wk * hi
    o_ref[:, :half] = lo_sum
    o_ref[:, half:] = hi_sum


def _combine(dest_flat, ys, x1, wt, rows=256):
    n, d = x1.shape
    rows = _tile(n, rows)
    return pl.pallas_call(
        functools.partial(_combine_kernel, rows=rows),
        grid_spec=pltpu.PrefetchScalarGridSpec(
            num_scalar_prefetch=1,
            grid=(n // rows,),
            in_specs=[pl.BlockSpec(memory_space=pl.ANY),
                      pl.BlockSpec((rows, d), lambda i, dest: (i, 0)),
                      pl.BlockSpec((rows, LANES), lambda i, dest: (i, 0))],
            out_specs=pl.BlockSpec((rows, d), lambda i, dest: (i, 0)),
            scratch_shapes=[pltpu.VMEM((TOP_K, rows, d // 2), U32), pltpu.SemaphoreType.DMA(())],
        ),
        out_shape=jax.ShapeDtypeStruct((n, d), F32),
        compiler_params=_params("arbitrary"),
        name="expert_combine",
    )(dest_flat, ys, x1, wt)


def _routed_experts(x1, norm_g, w_router, b_router, wg, bg, wu, bu, wd, bd, te=512):
    n, d = x1.shape
    n_exp = w_router.shape[1]
    hp, idx_p, wt_p, rank_p, cnt = _router(x1, norm_g, w_router, b_router)
    top_idx = idx_p[:, :TOP_K]
    rank = rank_p[:, :TOP_K]

    te = min(te, n * TOP_K)
    counts = cnt[0, :n_exp].astype(I32)
    padded = ((counts + te - 1) // te) * te
    pend = jnp.cumsum(padded)
    pstart = pend - padded
    n_tiles = (n * TOP_K) // te + n_exp
    n_used = (pend[-1] // te).astype(I32)
    tiles = jnp.arange(n_tiles, dtype=I32)
    tile_e = jnp.minimum(jnp.sum((pend[None, :] <= (tiles * te)[:, None]).astype(I32), axis=1), n_exp - 1)
    dest = (pstart[top_idx] + rank).reshape(n * TOP_K).astype(I32)
    partial_last = jnp.any((tiles[:, None] == (pend // te - 1)[None, :]) & ((counts % te) != 0)[None, :], axis=1)
    zero_flag = (partial_last | (tiles >= n_used)).astype(I32)

    xs = _dispatch(dest, zero_flag, hp, te)
    ys = _expert_mlp(tile_e, pstart // te, padded // te, n_used, xs, wg, bg, wu, bu, wd, bd, te)
    return _combine(dest, ys, x1, wt_p)


def kernel(x, mem, positions, norm1_g, w_in, q_lat_g, kv_lat_g, w_uq, w_ukv, qn_nope_g, qn_rope_g, kn_nope_g, kn_rope_g, conv_w, mem_norm_g, w_mem_kv, mem_qn_g, mem_kn_g, w_branch, w_out, norm2_g, w_router, b_router, w_gate, b_gate, w_up, b_up, w_down, b_down):
    b, s, d = x.shape
    n = b * s
    depth = w_in.shape[0]
    q_lora = w_uq.shape[1]
    kv_lora = w_ukv.shape[1]
    heads = w_ukv.shape[2] // (NOPE_DIM + V_DIM)
    mem_len = mem.shape[1]
    assert heads * V_DIM == d and d % (2 * LANES) == 0
    lat = q_lora + kv_lora
    kr_end = lat + ROPE_DIM
    assert w_in.shape[2] == kr_end + 7 * d

    pos = positions.reshape(n, 1).astype(I32)
    xf = x.reshape(n, d)
    mem_f = mem.reshape(b * mem_len, d)

    for l in range(depth):
        w_l = w_in[l]
        w_big = w_l[:, kr_end:].astype(BF16)
        w_small = jnp.concatenate(
            [w_l[:, :kr_end], w_l[:, lat + ROPE_DIM // 2:kr_end], w_l[:, lat:lat + ROPE_DIM // 2]], axis=1).astype(BF16)
        wq3 = w_uq[l].reshape(q_lora, heads, QK_DIM)
        wq = jnp.concatenate([wq3, wq3[..., NOPE_DIM + ROPE_DIM // 2:], wq3[..., NOPE_DIM:NOPE_DIM + ROPE_DIM // 2]],
                             axis=-1).reshape(q_lora, heads * HEAD_COLS).astype(BF16)
        wkv = w_ukv[l].astype(BF16)

        xn = _rmsnorm(xf, norm1_g[l])
        big = _matmul(xn, w_big, BF16, name="in_proj_wide")
        small = _matmul(xn, w_small, BF16, tn=w_small.shape[1], name="in_proj_latent")

        q, k, v = _mla_projections(small, pos, b, s, heads, q_lora, kv_lora, q_lat_g[l], kv_lat_g[l], wq, wkv,
                                   qn_nope_g[l], qn_rope_g[l], kn_nope_g[l], kn_rope_g[l])
        a_mla = _flash_attention(q, k, v)

        a_conv = _short_conv(big, conv_w[l], n, s, d)

        mem_n = _rmsnorm(mem_f, mem_norm_g[l])
        kvm = _matmul(mem_n, w_mem_kv[l].astype(BF16), BF16, name="mem_kv_proj")
        a_mem = _memory_attention(big, kvm, mem_qn_g[l], mem_kn_g[l], b, s, d, mem_len, q_seg=3)

        merged = _gated_merge(a_mla, a_conv, a_mem, w_branch[l].astype(BF16), big, gate_seg=4)
        x1 = _matmul(merged, w_out[l].astype(BF16), F32, res=xf, name="out_proj")

        xf = _routed_experts(x1, norm2_g[l], w_router[l], b_router[l],
                             w_gate[l], b_gate[l], w_up[l], b_up[l], w_down[l], b_down[l])
    return xf.reshape(b, s, d)
```

```python
import functools

import jax
import jax.numpy as jnp
from jax import lax
from jax.experimental import pallas as pl
from jax.experimental.pallas import tpu as pltpu

F32 = jnp.float32
BF16 = jnp.bfloat16
U32 = jnp.uint32
I32 = jnp.int32

EPS = 1e-6
NOPE_DIM = 128
ROPE_DIM = 64
V_DIM = 128
QK_DIM = NOPE_DIM + ROPE_DIM
HEAD_COLS = 256
MEM_HEADS = 4
TOP_K = 4
ROPE_THETA = 10000.0
SWIGLU_LIMIT = 7.0
SWIGLU_ALPHA = 1.702
LANES = 128
V7X_VMEM_LIMIT_BYTES = 56 * 1024 * 1024
HI_MASK = 0xFFFF0000
LOG2_E = 1.4426950408889634


def _params(*sem):
    return pltpu.CompilerParams(dimension_semantics=sem, vmem_limit_bytes=V7X_VMEM_LIMIT_BYTES)


def _tile(n, want):
    t = min(n, want)
    while n % t:
        t //= 2
    assert t >= 1
    return t


def _bits(x):
    return lax.bitcast_convert_type(x, U32)


def _pack_pair(lo, hi):
    lo_bits = _bits(lo.astype(BF16).astype(F32)) >> 16
    hi_bits = _bits(hi.astype(BF16).astype(F32)) & jnp.uint32(HI_MASK)
    return lo_bits | hi_bits


def _pack_halves(y):
    half = y.shape[1] // 2
    return _pack_pair(y[:, :half], y[:, half:])


def _unpack_halves(p):
    lo = lax.bitcast_convert_type(p << 16, F32)
    hi = lax.bitcast_convert_type(p & jnp.uint32(HI_MASK), F32)
    return lo, hi


def _rmsnorm_kernel(x_ref, g_ref, o_ref):
    x = x_ref[...].astype(F32)
    r = lax.rsqrt(jnp.mean(x * x, axis=-1, keepdims=True) + EPS)
    o_ref[...] = (x * r * g_ref[...]).astype(o_ref.dtype)


def _rmsnorm(x, g, tm=512):
    m, d = x.shape
    tm = _tile(m, tm)
    return pl.pallas_call(
        _rmsnorm_kernel,
        grid=(m // tm,),
        in_specs=[pl.BlockSpec((tm, d), lambda i: (i, 0)), pl.BlockSpec((1, d), lambda i: (0, 0))],
        out_specs=pl.BlockSpec((tm, d), lambda i: (i, 0)),
        out_shape=jax.ShapeDtypeStruct((m, d), BF16),
        compiler_params=_params("parallel"),
        name="rmsnorm",
    )(x, g.reshape(1, d))


def _window_cast_kernel(w_ref, o_ref, *, start):
    o_ref[...] = w_ref[:, start:start + o_ref.shape[1]].astype(o_ref.dtype)


def _window_cast(w, start, width, tr=64):
    rows, cols = w.shape
    tr = _tile(rows, tr)
    return pl.pallas_call(
        functools.partial(_window_cast_kernel, start=start),
        grid=(rows // tr,),
        in_specs=[pl.BlockSpec((tr, cols), lambda i: (i, 0))],
        out_specs=pl.BlockSpec((tr, width), lambda i: (i, 0)),
        out_shape=jax.ShapeDtypeStruct((rows, width), BF16),
        compiler_params=_params("parallel"),
        name="weight_window_cast",
    )(w)


def _mm_kernel(x_ref, w_ref, o_ref):
    o_ref[...] = jnp.dot(x_ref[...], w_ref[...], preferred_element_type=F32).astype(o_ref.dtype)


def _mm_res_kernel(x_ref, w_ref, r_ref, o_ref):
    o_ref[...] = r_ref[...] + jnp.dot(x_ref[...], w_ref[...], preferred_element_type=F32)


def _matmul(x, w, out_dtype, tm=1024, tn=1024, res=None, name="matmul"):
    m, k = x.shape
    n = w.shape[1]
    tm, tn = _tile(m, tm), _tile(n, tn)
    in_specs = [pl.BlockSpec((tm, k), lambda i, j: (i, 0)), pl.BlockSpec((k, tn), lambda i, j: (0, j))]
    args = [x, w]
    body = _mm_kernel
    if res is not None:
        in_specs.append(pl.BlockSpec((tm, tn), lambda i, j: (i, j)))
        args.append(res)
        body = _mm_res_kernel
    return pl.pallas_call(
        body,
        grid=(m // tm, n // tn),
        in_specs=in_specs,
        out_specs=pl.BlockSpec((tm, tn), lambda i, j: (i, j)),
        out_shape=jax.ShapeDtypeStruct((m, n), out_dtype),
        compiler_params=_params("parallel", "parallel"),
        name=name,
    )(*args)


def _rope_tables(pos_ref, invf_ref, sgn_ref, cos_s, sin_s):
    ang = pos_ref[...].astype(F32) * invf_ref[...]
    cos_s[...] = jnp.cos(ang)
    sin_s[...] = jnp.sin(ang) * sgn_ref[...]


def _rope_pair(slab, gain, cos, sin):
    r = lax.rsqrt(jnp.mean(slab * slab, axis=-1, keepdims=True) + EPS)
    slab = slab * r * gain
    return slab * cos + pltpu.roll(slab, ROPE_DIM, 1) * sin


def _qproj_kernel(cq_ref, pos_ref, invf_ref, sgn_ref, glat_ref, gq_ref, w_ref, o_ref, cqn_s, cos_s, sin_s, *, scale):
    @pl.when(pl.program_id(1) == 0)
    def _():
        c = cq_ref[...].astype(F32)
        r = lax.rsqrt(jnp.mean(c * c, axis=-1, keepdims=True) + EPS)
        cqn_s[...] = (c * r * glat_ref[...]).astype(BF16)
        _rope_tables(pos_ref, invf_ref, sgn_ref, cos_s, sin_s)

    for g in range(o_ref.shape[0]):
        q = jnp.dot(cqn_s[...], w_ref[:, g * HEAD_COLS:(g + 1) * HEAD_COLS], preferred_element_type=F32)
        nope = q[:, :NOPE_DIM]
        rn = lax.rsqrt(jnp.mean(nope * nope, axis=-1, keepdims=True) + EPS)
        nope = nope * rn * gq_ref[:, :NOPE_DIM]
        rot = _rope_pair(q[:, NOPE_DIM:], gq_ref[:, NOPE_DIM:], cos_s[...], sin_s[...])
        o_ref[g, :, :NOPE_DIM] = (nope * scale).astype(o_ref.dtype)
        o_ref[g, :, NOPE_DIM:] = (rot[:, :ROPE_DIM] * scale).astype(o_ref.dtype)


def _kvproj_kernel(ckv_ref, kr_ref, pos_ref, invf_ref, sgn_ref, glat_ref, gkr_ref, gkn_ref, w_ref,
                   k_ref, v_ref, ckvn_s, kr_s, cos_s, sin_s):
    @pl.when(pl.program_id(1) == 0)
    def _():
        c = ckv_ref[...].astype(F32)
        r = lax.rsqrt(jnp.mean(c * c, axis=-1, keepdims=True) + EPS)
        ckvn_s[...] = (c * r * glat_ref[...]).astype(BF16)
        _rope_tables(pos_ref, invf_ref, sgn_ref, cos_s, sin_s)
        kr_s[...] = _rope_pair(kr_ref[...].astype(F32), gkr_ref[...], cos_s[...], sin_s[...]).astype(BF16)

    for g in range(k_ref.shape[0]):
        kv = jnp.dot(ckvn_s[...], w_ref[:, g * HEAD_COLS:(g + 1) * HEAD_COLS], preferred_element_type=F32)
        kn = kv[:, :NOPE_DIM]
        rn = lax.rsqrt(jnp.mean(kn * kn, axis=-1, keepdims=True) + EPS)
        k_ref[g, :, :NOPE_DIM] = (kn * rn * gkn_ref[...]).astype(k_ref.dtype)
        k_ref[g, :, NOPE_DIM:] = kr_s[:, :ROPE_DIM]
        v_ref[g] = kv[:, NOPE_DIM:].astype(v_ref.dtype)


def _rope_consts():
    inv_freq = ROPE_THETA ** (-jnp.arange(0, ROPE_DIM, 2, dtype=F32) / ROPE_DIM)
    invf = jnp.tile(inv_freq, LANES // (ROPE_DIM // 2)).reshape(1, LANES)
    half = jnp.concatenate([-jnp.ones((ROPE_DIM // 2,), F32), jnp.ones((ROPE_DIM // 2,), F32)])
    sgn = jnp.tile(half, LANES // ROPE_DIM).reshape(1, LANES)
    return invf, sgn


def _swap_halves(g):
    return jnp.concatenate([g[..., ROPE_DIM // 2:], g[..., :ROPE_DIM // 2]], axis=-1)


def _mla_projections(small, pos, b, s, heads, q_lora, kv_lora, q_lat_g, kv_lat_g, wq, wkv,
                     qn_nope_g, qn_rope_g, kn_nope_g, kn_rope_g, tm=512, hps=8):
    n = b * s
    tm = _tile(s, tm)
    spb = s // tm
    invf, sgn = _rope_consts()
    scale = float(QK_DIM) ** -0.5 * LOG2_E
    gq = jnp.concatenate([qn_nope_g, qn_rope_g, _swap_halves(qn_rope_g)]).reshape(1, HEAD_COLS)
    gkr = jnp.concatenate([kn_rope_g, _swap_halves(kn_rope_g)]).reshape(1, LANES)
    hps = _tile(heads, hps)
    row = lambda i, h: (i, 0)
    const = lambda i, h: (0, 0)
    head_out = lambda i, h: (i // spb, h, i % spb, 0)
    assert q_lora % kv_lora == 0 and kv_lora % LANES == 0
    ckv_blk = q_lora // kv_lora
    kr_blk = (q_lora + kv_lora) // LANES

    q = pl.pallas_call(
        functools.partial(_qproj_kernel, scale=scale),
        grid=(n // tm, heads // hps),
        in_specs=[pl.BlockSpec((tm, q_lora), row), pl.BlockSpec((tm, 1), row),
                  pl.BlockSpec((1, LANES), const), pl.BlockSpec((1, LANES), const),
                  pl.BlockSpec((1, q_lora), const), pl.BlockSpec((1, HEAD_COLS), const),
                  pl.BlockSpec((q_lora, hps * HEAD_COLS), lambda i, h: (0, h))],
        out_specs=pl.BlockSpec((None, hps, tm, QK_DIM), head_out),
        out_shape=jax.ShapeDtypeStruct((b, heads, s, QK_DIM), BF16),
        scratch_shapes=[pltpu.VMEM((tm, q_lora), BF16), pltpu.VMEM((tm, LANES), F32), pltpu.VMEM((tm, LANES), F32)],
        compiler_params=_params("parallel", "arbitrary"),
        name="mla_q_proj",
    )(small, pos, invf, sgn, q_lat_g.reshape(1, q_lora), gq, wq)

    k, v = pl.pallas_call(
        _kvproj_kernel,
        grid=(n // tm, heads // hps),
        in_specs=[pl.BlockSpec((tm, kv_lora), lambda i, h: (i, ckv_blk)),
                  pl.BlockSpec((tm, LANES), lambda i, h: (i, kr_blk)),
                  pl.BlockSpec((tm, 1), row),
                  pl.BlockSpec((1, LANES), const), pl.BlockSpec((1, LANES), const),
                  pl.BlockSpec((1, kv_lora), const), pl.BlockSpec((1, LANES), const),
                  pl.BlockSpec((1, NOPE_DIM), const),
                  pl.BlockSpec((kv_lora, hps * HEAD_COLS), lambda i, h: (0, h))],
        out_specs=[pl.BlockSpec((None, hps, tm, QK_DIM), head_out),
                   pl.BlockSpec((None, hps, tm, V_DIM), head_out)],
        out_shape=[jax.ShapeDtypeStruct((b, heads, s, QK_DIM), BF16),
                   jax.ShapeDtypeStruct((b, heads, s, V_DIM), BF16)],
        scratch_shapes=[pltpu.VMEM((tm, kv_lora), BF16), pltpu.VMEM((tm, LANES), BF16),
                        pltpu.VMEM((tm, LANES), F32), pltpu.VMEM((tm, LANES), F32)],
        compiler_params=_params("parallel", "arbitrary"),
        name="mla_kv_proj",
    )(small, small, pos, invf, sgn, kv_lat_g.reshape(1, kv_lora), gkr, kn_nope_g.reshape(1, NOPE_DIM), wkv)
    return q, k, v


def _flash_kernel(q_ref, k_ref, v_ref, o_ref, s_a, s_b, m_s, l_s, acc_s, *, tq, tk):
    n_q = q_ref.shape[0] // tq

    def scores(j, dst, q0, r0=0):
        ks = k_ref[pl.ds(pl.multiple_of(j * tk, tk), tk), :]
        dst[r0:, :] = lax.dot_general(q_ref[q0 + r0:q0 + tq, :], ks, (((1,), (1,)), ((), ())),
                                      preferred_element_type=F32)

    def soft_pv(src, j, q0, r0=0, diagonal=False):
        rows = slice(q0 + r0, q0 + tq)
        s = src[r0:, :]
        if diagonal:
            row = lax.broadcasted_iota(I32, s.shape, 0)
            col = lax.broadcasted_iota(I32, s.shape, 1)
            s = jnp.where(col <= row, s, -jnp.inf)
        m_prev = m_s[rows, :]
        m_new = jnp.maximum(m_prev, jnp.max(s, axis=-1, keepdims=True))
        alpha = jnp.exp2(m_prev - m_new)
        ps = [jnp.exp2(s[:, c * LANES:(c + 1) * LANES] - m_new) for c in range(tk // LANES)]
        part = ps[0]
        for x in ps[1:]:
            part = part + x
        l_s[rows, :] = alpha * l_s[rows, :] + part
        m_s[rows, :] = m_new
        vs = v_ref[pl.ds(pl.multiple_of(j * tk, tk), tk), :]
        p = jnp.concatenate(ps, axis=1).astype(BF16)
        acc_s[rows, :] = alpha * acc_s[rows, :] + jnp.dot(p, vs, preferred_element_type=F32)

    m_s[...] = jnp.full_like(m_s, -jnp.inf)
    l_s[...] = jnp.zeros_like(l_s)
    acc_s[...] = jnp.zeros_like(acc_s)

    for i in range(n_q):
        q0 = i * tq
        scores(0, s_a, q0)

        def pair(jj, c, q0=q0):
            j = 2 * jj
            scores(j + 1, s_b, q0)
            soft_pv(s_a, j, q0)
            scores(j + 2, s_a, q0)
            soft_pv(s_b, j + 1, q0)
            return c

        if i:
            lax.fori_loop(0, i, pair, 0)
        scores(2 * i + 1, s_b, q0, tk)
        soft_pv(s_a, 2 * i, q0, 0, True)
        soft_pv(s_b, 2 * i + 1, q0, tk, True)
        l = jnp.sum(l_s[q0:q0 + tq, :], axis=-1, keepdims=True)
        o_ref[q0:q0 + tq, :] = (acc_s[q0:q0 + tq, :] / l).astype(o_ref.dtype)


def _flash_attention(q, k, v, tq=1024):
    b, heads, s, _ = q.shape
    tq = _tile(s, tq)
    tk = tq // 2
    assert tk % LANES == 0
    whole = lambda bi, h: (bi, h, 0, 0)
    return pl.pallas_call(
        functools.partial(_flash_kernel, tq=tq, tk=tk),
        grid=(b, heads),
        in_specs=[pl.BlockSpec((None, None, s, QK_DIM), whole), pl.BlockSpec((None, None, s, QK_DIM), whole),
                  pl.BlockSpec((None, None, s, V_DIM), whole)],
        out_specs=pl.BlockSpec((s, V_DIM), lambda bi, h: (bi, h)),
        out_shape=jax.ShapeDtypeStruct((b * s, heads * V_DIM), BF16),
        scratch_shapes=[pltpu.VMEM((tq, tk), F32), pltpu.VMEM((tq, tk), F32),
                        pltpu.VMEM((s, LANES), F32), pltpu.VMEM((s, LANES), F32), pltpu.VMEM((s, V_DIM), F32)],
        compiler_params=_params("parallel", "parallel"),
        name="mla_flash_attention",
    )(q, k, v)


def _conv_kernel(cb_ref, cc_ref, ch_ref, pc_ref, ph_ref, w_ref, o_ref, *, tiles_per_seq):
    first = (pl.program_id(0) % tiles_per_seq) == 0
    u = cc_ref[...].astype(F32) * ch_ref[...].astype(F32)
    halo = pc_ref[...].astype(F32) * ph_ref[...].astype(F32)
    halo = jnp.where(first, 0.0, halo)
    cat = jnp.concatenate([halo, u], axis=0)
    n_halo = halo.shape[0]
    u1 = pltpu.roll(cat, 1, 0)[n_halo:]
    u2 = pltpu.roll(cat, 2, 0)[n_halo:]
    y = w_ref[0:1, :] * u2 + w_ref[1:2, :] * u1 + w_ref[2:3, :] * u
    o_ref[...] = (cb_ref[...].astype(F32) * y).astype(o_ref.dtype)


def _short_conv(big, conv_w, n, s, d, ts=512, tc=1024):
    ts, tc = _tile(s, ts), _tile(d, tc)
    halo = 8
    cpb = d // tc
    seg = lambda k: (lambda i, j: (i, k * cpb + j))
    prev = lambda k: (lambda i, j: (jnp.maximum(i * (ts // halo) - 1, 0), k * cpb + j))
    return pl.pallas_call(
        functools.partial(_conv_kernel, tiles_per_seq=s // ts),
        grid=(n // ts, cpb),
        in_specs=[pl.BlockSpec((ts, tc), seg(0)), pl.BlockSpec((ts, tc), seg(1)), pl.BlockSpec((ts, tc), seg(2)),
                  pl.BlockSpec((halo, tc), prev(1)), pl.BlockSpec((halo, tc), prev(2)),
                  pl.BlockSpec((conv_w.shape[0], tc), lambda i, j: (0, j))],
        out_specs=pl.BlockSpec((ts, tc), lambda i, j: (i, j)),
        out_shape=jax.ShapeDtypeStruct((n, d), BF16),
        compiler_params=_params("parallel", "parallel"),
        name="short_conv",
    )(big, big, big, big, big, conv_w)


def _memattn_kernel(q_ref, k_ref, v_ref, gq_ref, gk_ref, o_ref, kn_s, *, scale):
    @pl.when(pl.program_id(2) == 0)
    def _():
        k = k_ref[...].astype(F32)
        r = lax.rsqrt(jnp.mean(k * k, axis=-1, keepdims=True) + EPS)
        kn_s[...] = (k * r * gk_ref[...]).astype(BF16)

    q = q_ref[...].astype(F32)
    r = lax.rsqrt(jnp.mean(q * q, axis=-1, keepdims=True) + EPS)
    qn = (q * (r * scale) * gq_ref[...]).astype(BF16)
    s = lax.dot_general(qn, kn_s[...], (((1,), (1,)), ((), ())), preferred_element_type=F32)
    m = jnp.max(s, axis=-1, keepdims=True)
    p = jnp.exp(s - m)
    l = jnp.sum(p, axis=-1, keepdims=True)
    o = jnp.dot(p.astype(BF16), v_ref[...], preferred_element_type=F32)
    o_ref[...] = (o / l).astype(o_ref.dtype)


def _memory_attention(big, kvm, gq, gk, b, s, d, mem_len, q_seg, ts=512):
    hd = d // MEM_HEADS
    ts = _tile(s, ts)
    spb = s // ts
    return pl.pallas_call(
        functools.partial(_memattn_kernel, scale=float(hd) ** -0.5),
        grid=(b, MEM_HEADS, spb),
        in_specs=[pl.BlockSpec((ts, hd), lambda bi, h, i: (bi * spb + i, q_seg * MEM_HEADS + h)),
                  pl.BlockSpec((mem_len, hd), lambda bi, h, i: (bi, h)),
                  pl.BlockSpec((mem_len, hd), lambda bi, h, i: (bi, MEM_HEADS + h)),
                  pl.BlockSpec((1, hd), lambda bi, h, i: (0, 0)),
                  pl.BlockSpec((1, hd), lambda bi, h, i: (0, 0))],
        out_specs=pl.BlockSpec((ts, hd), lambda bi, h, i: (bi * spb + i, h)),
        out_shape=jax.ShapeDtypeStruct((b * s, d), BF16),
        scratch_shapes=[pltpu.VMEM((mem_len, hd), BF16)],
        compiler_params=_params("parallel", "parallel", "arbitrary"),
        name="memory_attention",
    )(big, kvm, kvm, gq.reshape(1, hd), gk.reshape(1, hd))


def _merge_kernel(a0_ref, a1_ref, a2_ref, w_ref, g_ref, o_ref, acc_s):
    br = pl.program_id(2)
    gate = jax.nn.sigmoid(g_ref[...].astype(F32))

    @pl.when(br == 0)
    def _():
        acc_s[...] = gate * jnp.dot(a0_ref[...], w_ref[...], preferred_element_type=F32)

    @pl.when(br == 1)
    def _():
        acc_s[...] += gate * jnp.dot(a1_ref[...], w_ref[...], preferred_element_type=F32)

    @pl.when(br == 2)
    def _():
        o_ref[...] = (acc_s[...] + gate * jnp.dot(a2_ref[...], w_ref[...], preferred_element_type=F32)).astype(o_ref.dtype)


def _gated_merge(a_mla, a_conv, a_mem, w_branch, big, gate_seg, tm=512, tn=1024):
    n, d = a_mla.shape
    tm, tn = _tile(n, tm), _tile(d, tn)
    cpb = d // tn
    lhs = pl.BlockSpec((tm, d), lambda i, j, br: (i, 0))
    return pl.pallas_call(
        _merge_kernel,
        grid=(n // tm, cpb, 3),
        in_specs=[lhs, lhs, lhs,
                  pl.BlockSpec((None, d, tn), lambda i, j, br: (br, 0, j)),
                  pl.BlockSpec((tm, tn), lambda i, j, br: (i, (gate_seg + br) * cpb + j))],
        out_specs=pl.BlockSpec((tm, tn), lambda i, j, br: (i, j)),
        out_shape=jax.ShapeDtypeStruct((n, d), BF16),
        scratch_shapes=[pltpu.VMEM((tm, tn), F32)],
        compiler_params=_params("parallel", "parallel", "arbitrary"),
        name="gated_merge",
    )(a_mla, a_conv, a_mem, w_branch, big)


def _router_kernel(x_ref, g_ref, wr_ref, br_ref, hp_ref, idx_ref, wt_ref, rank_ref, cnt_ref, carry_s, *, n_exp):
    tm = x_ref.shape[0]

    @pl.when(pl.program_id(0) == 0)
    def _():
        carry_s[...] = jnp.zeros_like(carry_s)

    x = x_ref[...]
    r = lax.rsqrt(jnp.mean(x * x, axis=-1, keepdims=True) + EPS)
    h = x * r * g_ref[...]
    hp_ref[...] = _pack_halves(h)

    logits = jnp.dot(h, wr_ref[...], preferred_element_type=F32, precision=lax.Precision.HIGHEST) + br_ref[...]
    lane = lax.broadcasted_iota(I32, (tm, n_exp), 1)
    out_lane = lax.broadcasted_iota(I32, (tm, LANES), 1)
    work = logits
    vals, hots = [], []
    idx_out = jnp.zeros((tm, LANES), I32)
    for k in range(TOP_K):
        m = jnp.max(work, axis=-1, keepdims=True)
        idx = jnp.min(jnp.where(work == m, lane, n_exp), axis=-1, keepdims=True)
        hot = lane == idx
        vals.append(m)
        hots.append(hot)
        idx_out = jnp.where(out_lane == k, idx, idx_out)
        work = jnp.where(hot, -jnp.inf, work)
    idx_ref[...] = idx_out

    exps = [jnp.exp(v - vals[0]) for v in vals]
    denom = exps[0]
    for e in exps[1:]:
        denom = denom + e
    wt_out = jnp.zeros((tm, LANES), F32)
    for k in range(TOP_K):
        wt_out = jnp.where(out_lane == k, exps[k] / denom, wt_out)
    wt_ref[...] = wt_out

    member = jnp.zeros((tm, n_exp), F32)
    for hot in hots:
        member = member + hot.astype(F32)
    tri = (lax.broadcasted_iota(I32, (tm, tm), 1) < lax.broadcasted_iota(I32, (tm, tm), 0)).astype(BF16)
    prefix = jnp.dot(tri, member.astype(BF16), preferred_element_type=F32) + carry_s[0:1, 0:n_exp]
    rank_out = jnp.zeros((tm, LANES), I32)
    for k in range(TOP_K):
        rk = jnp.sum(jnp.where(hots[k], prefix, 0.0), axis=-1, keepdims=True).astype(I32)
        rank_out = jnp.where(out_lane == k, rk, rank_out)
    rank_ref[...] = rank_out

    carry_s[0:1, 0:n_exp] = carry_s[0:1, 0:n_exp] + jnp.sum(member, axis=0, keepdims=True)
    cnt_ref[...] = carry_s[...]


def _router(x1, g, w_router, b_router, tm=512):
    n, d = x1.shape
    n_exp = w_router.shape[1]
    assert n_exp <= LANES
    tm = _tile(n, tm)
    row = lambda i: (i, 0)
    const = lambda i: (0, 0)
    return pl.pallas_call(
        functools.partial(_router_kernel, n_exp=n_exp),
        grid=(n // tm,),
        in_specs=[pl.BlockSpec((tm, d), row), pl.BlockSpec((1, d), const),
                  pl.BlockSpec((d, n_exp), const), pl.BlockSpec((1, n_exp), const)],
        out_specs=[pl.BlockSpec((tm, d // 2), row), pl.BlockSpec((tm, LANES), row), pl.BlockSpec((tm, LANES), row),
                   pl.BlockSpec((tm, LANES), row), pl.BlockSpec((8, LANES), const)],
        out_shape=[jax.ShapeDtypeStruct((n, d // 2), U32), jax.ShapeDtypeStruct((n, LANES), I32),
                   jax.ShapeDtypeStruct((n, LANES), F32), jax.ShapeDtypeStruct((n, LANES), I32),
                   jax.ShapeDtypeStruct((8, LANES), F32)],
        scratch_shapes=[pltpu.VMEM((8, LANES), F32)],
        compiler_params=_params("arbitrary"),
        name="router",
    )(x1, g.reshape(1, d), w_router, b_router.reshape(1, n_exp))


def _dispatch_kernel(dest_ref, zflag_ref, hp_ref, xs_ref, zeros_s, sem, zsem, *, rows):
    base = pl.program_id(0) * rows
    te = zeros_s.shape[0]

    @pl.when(pl.program_id(0) == 0)
    def _():
        zeros_s[...] = jnp.zeros_like(zeros_s)

        def fill(t):
            return pltpu.make_async_copy(zeros_s, xs_ref.at[pl.ds(pl.multiple_of(t * te, te), te)], zsem)

        def start(t, c):
            @pl.when(zflag_ref[t] != 0)
            def _():
                fill(t).start()
            return c

        def finish(t, c):
            @pl.when(zflag_ref[t] != 0)
            def _():
                fill(t).wait()
            return c

        lax.fori_loop(0, zflag_ref.shape[0], start, 0)
        lax.fori_loop(0, zflag_ref.shape[0], finish, 0)

    def issue(r, c):
        for k in range(TOP_K):
            d = dest_ref[(base + r) * TOP_K + k]
            pltpu.make_async_copy(hp_ref.at[pl.ds(r, 1)], xs_ref.at[pl.ds(d, 1)], sem).start()
        return c

    lax.fori_loop(0, rows, issue, 0)

    def drain(r, c):
        for k in range(TOP_K):
            pltpu.make_async_copy(hp_ref.at[pl.ds(0, 1)], xs_ref.at[pl.ds(0, 1)], sem).wait()
        return c

    lax.fori_loop(0, rows, drain, 0)


def _dispatch(dest_flat, zero_flag, hp, te, rows=256):
    n, half = hp.shape
    rows = _tile(n, rows)
    n_slots = zero_flag.shape[0] * te
    return pl.pallas_call(
        functools.partial(_dispatch_kernel, rows=rows),
        grid_spec=pltpu.PrefetchScalarGridSpec(
            num_scalar_prefetch=2,
            grid=(n // rows,),
            in_specs=[pl.BlockSpec((rows, half), lambda i, dest, zflag: (i, 0))],
            out_specs=pl.BlockSpec(memory_space=pl.ANY),
            scratch_shapes=[pltpu.VMEM((te, half), U32), pltpu.SemaphoreType.DMA(()), pltpu.SemaphoreType.DMA(())],
        ),
        out_shape=jax.ShapeDtypeStruct((n_slots, half), U32),
        compiler_params=_params("arbitrary"),
        name="expert_dispatch",
    )(dest_flat, zero_flag, hp)


def _group_items(tile_e, first_tile, tiles_of, n_used, n_blocks, n_tiles):
    n_items = n_used * n_blocks
    w = jnp.arange(n_tiles * n_blocks, dtype=I32)
    wc = jnp.minimum(w, n_items - 1)
    e = tile_e[wc // n_blocks]
    first, count = first_tile[e], jnp.maximum(tiles_of[e], 1)
    r = wc - first * n_blocks
    in_tile, in_blk = first + r % count, r // count
    z = jnp.maximum(w - n_items, 0)
    live = w < n_items
    out_tile = jnp.where(live, in_tile, n_used + z // n_blocks)
    out_blk = jnp.where(live, in_blk, z % n_blocks)
    starts = (live & (r % count == 0)).astype(I32)
    return in_tile, in_blk, e, starts, out_tile, out_blk, n_items.reshape(1)


def _expert_up_kernel(tile_ref, blk_ref, exp_ref, start_ref, ot_ref, ob_ref, ni_ref,
                      xs_ref, wg_ref, bg_ref, wu_ref, bu_ref, act_ref, xb_s, wg_s, wu_s):
    w = pl.program_id(0)
    live = w < ni_ref[0]
    half = xs_ref.shape[1]

    @pl.when(start_ref[w] != 0)
    def _():
        wg_s[...] = wg_ref[...].astype(BF16)
        wu_s[...] = wu_ref[...].astype(BF16)

    @pl.when(live)
    def _():
        lo, hi = _unpack_halves(xs_ref[...])
        xb_s[:, :half] = lo.astype(BF16)
        xb_s[:, half:] = hi.astype(BF16)
        xb = xb_s[...]
        g = jnp.dot(xb, wg_s[...], preferred_element_type=F32) + bg_ref[...]
        u = jnp.dot(xb, wu_s[...], preferred_element_type=F32) + bu_ref[...]
        g = jnp.minimum(g, SWIGLU_LIMIT)
        u = jnp.clip(u, -SWIGLU_LIMIT, SWIGLU_LIMIT)
        act_ref[...] = (g * jax.nn.sigmoid(SWIGLU_ALPHA * g) * (u + 1.0)).astype(act_ref.dtype)

    @pl.when(jnp.logical_not(live))
    def _():
        act_ref[...] = jnp.zeros_like(act_ref)


def _expert_down_kernel(tile_ref, blk_ref, exp_ref, start_ref, ot_ref, ob_ref, ni_ref,
                        act_ref, wdl_ref, wdh_ref, bdl_ref, bdh_ref, o_ref, wl_s, wh_s):
    w = pl.program_id(0)
    live = w < ni_ref[0]

    @pl.when(start_ref[w] != 0)
    def _():
        wl_s[...] = wdl_ref[...].astype(BF16)
        wh_s[...] = wdh_ref[...].astype(BF16)

    @pl.when(live)
    def _():
        a = act_ref[...]
        lo = jnp.dot(a, wl_s[...], preferred_element_type=F32) + bdl_ref[...]
        hi = jnp.dot(a, wh_s[...], preferred_element_type=F32) + bdh_ref[...]
        o_ref[...] = _pack_pair(lo, hi)

    @pl.when(jnp.logical_not(live))
    def _():
        o_ref[...] = jnp.zeros_like(o_ref)


def _expert_mlp(tile_e, first_tile, tiles_of, n_used, xs, wg, bg, wu, bu, wd, bd, te, tcu=256, tnd=1024):
    n_slots, half = xs.shape
    d = 2 * half
    n_exp, _, de = wg.shape
    tcu, tnd = _tile(de, tcu), _tile(half, tnd)
    n_up, n_down = de // tcu, half // tnd
    n_tiles = n_slots // te
    n_pre = 7

    in_row = lambda w, tl, bk, ex, st, ot, ob, ni: (tl[w], 0)
    w_blk = lambda w, tl, bk, ex, st, ot, ob, ni: (ex[w], 0, bk[w])
    out_blk = lambda w, tl, bk, ex, st, ot, ob, ni: (ot[w], ob[w])

    up_items = _group_items(tile_e, first_tile, tiles_of, n_used, n_up, n_tiles)
    act = pl.pallas_call(
        _expert_up_kernel,
        grid_spec=pltpu.PrefetchScalarGridSpec(
            num_scalar_prefetch=n_pre,
            grid=(n_tiles * n_up,),
            in_specs=[pl.BlockSpec((te, half), in_row),
                      pl.BlockSpec((None, d, tcu), w_blk), pl.BlockSpec((None, 1, tcu), w_blk),
                      pl.BlockSpec((None, d, tcu), w_blk), pl.BlockSpec((None, 1, tcu), w_blk)],
            out_specs=pl.BlockSpec((te, tcu), out_blk),
            scratch_shapes=[pltpu.VMEM((te, d), BF16), pltpu.VMEM((d, tcu), BF16), pltpu.VMEM((d, tcu), BF16)],
        ),
        out_shape=jax.ShapeDtypeStruct((n_slots, de), BF16),
        compiler_params=_params("arbitrary"),
        name="expert_gate_up",
    )(*up_items, xs, wg, bg.reshape(n_exp, 1, de), wu, bu.reshape(n_exp, 1, de))

    hi_blk = lambda w, tl, bk, ex, st, ot, ob, ni: (ex[w], 0, n_down + bk[w])
    bd3 = bd.reshape(n_exp, 1, d)
    down_items = _group_items(tile_e, first_tile, tiles_of, n_used, n_down, n_tiles)
    return pl.pallas_call(
        _expert_down_kernel,
        grid_spec=pltpu.PrefetchScalarGridSpec(
            num_scalar_prefetch=n_pre,
            grid=(n_tiles * n_down,),
            in_specs=[pl.BlockSpec((te, de), in_row),
                      pl.BlockSpec((None, de, tnd), w_blk), pl.BlockSpec((None, de, tnd), hi_blk),
                      pl.BlockSpec((None, 1, tnd), w_blk), pl.BlockSpec((None, 1, tnd), hi_blk)],
            out_specs=pl.BlockSpec((te, tnd), out_blk),
            scratch_shapes=[pltpu.VMEM((de, tnd), BF16), pltpu.VMEM((de, tnd), BF16)],
        ),
        out_shape=jax.ShapeDtypeStruct((n_slots, half), U32),
        compiler_params=_params("arbitrary"),
        name="expert_down",
    )(*down_items, act, wd, wd, bd3, bd3)


def _combine_kernel(dest_ref, ys_ref, x_ref, wt_ref, o_ref, gbuf, sem, *, rows):
    base = pl.program_id(0) * rows
    half = gbuf.shape[2]

    def issue(r, c):
        for k in range(TOP_K):
            d = dest_ref[(base + r) * TOP_K + k]
            pltpu.make_async_copy(ys_ref.at[pl.ds(d, 1)], gbuf.at[k, pl.ds(r, 1)], sem).start()
        return c

    lax.fori_loop(0, rows, issue, 0)

    def drain(r, c):
        for k in range(TOP_K):
            pltpu.make_async_copy(ys_ref.at[pl.ds(0, 1)], gbuf.at[k, pl.ds(0, 1)], sem).wait()
        return c

    lax.fori_loop(0, rows, drain, 0)

    lo_sum = x_ref[:, :half]
    hi_sum = x_ref[:, half:]
    for k in range(TOP_K):
        lo, hi = _unpack_halves(gbuf[k])
        wk = wt_ref[:, k:k + 1]
        lo_sum = lo_sum + wk * lo
        hi_sum = hi_sum + wk * hi
    o_ref[:, :half] = lo_sum
    o_ref[:, half:] = hi_sum


def _combine(dest_flat, ys, x1, wt, rows=256):
    n, d = x1.shape
    rows = _tile(n, rows)
    return pl.pallas_call(
        functools.partial(_combine_kernel, rows=rows),
        grid_spec=pltpu.PrefetchScalarGridSpec(
            num_scalar_prefetch=1,
            grid=(n // rows,),
            in_specs=[pl.BlockSpec(memory_space=pl.ANY),
                      pl.BlockSpec((rows, d), lambda i, dest: (i, 0)),
                      pl.BlockSpec((rows, LANES), lambda i, dest: (i, 0))],
            out_specs=pl.BlockSpec((rows, d), lambda i, dest: (i, 0)),
            scratch_shapes=[pltpu.VMEM((TOP_K, rows, d // 2), U32), pltpu.SemaphoreType.DMA(())],
        ),
        out_shape=jax.ShapeDtypeStruct((n, d), F32),
        compiler_params=_params("arbitrary"),
        name="expert_combine",
    )(dest_flat, ys, x1, wt)


def _routed_experts(x1, norm_g, w_router, b_router, wg, bg, wu, bu, wd, bd, te=512):
    n, d = x1.shape
    n_exp = w_router.shape[1]
    hp, idx_p, wt_p, rank_p, cnt = _router(x1, norm_g, w_router, b_router)
    top_idx = idx_p[:, :TOP_K]
    rank = rank_p[:, :TOP_K]

    te = min(te, n * TOP_K)
    counts = cnt[0, :n_exp].astype(I32)
    padded = ((counts + te - 1) // te) * te
    pend = jnp.cumsum(padded)
    pstart = pend - padded
    n_tiles = (n * TOP_K) // te + n_exp
    n_used = (pend[-1] // te).astype(I32)
    tiles = jnp.arange(n_tiles, dtype=I32)
    tile_e = jnp.minimum(jnp.sum((pend[None, :] <= (tiles * te)[:, None]).astype(I32), axis=1), n_exp - 1)
    dest = (pstart[top_idx] + rank).reshape(n * TOP_K).astype(I32)
    partial_last = jnp.any((tiles[:, None] == (pend // te - 1)[None, :]) & ((counts % te) != 0)[None, :], axis=1)
    zero_flag = (partial_last | (tiles >= n_used)).astype(I32)

    xs = _dispatch(dest, zero_flag, hp, te)
    ys = _expert_mlp(tile_e, pstart // te, padded // te, n_used, xs, wg, bg, wu, bu, wd, bd, te)
    return _combine(dest, ys, x1, wt_p)


def kernel(x, mem, positions, norm1_g, w_in, q_lat_g, kv_lat_g, w_uq, w_ukv, qn_nope_g, qn_rope_g, kn_nope_g, kn_rope_g, conv_w, mem_norm_g, w_mem_kv, mem_qn_g, mem_kn_g, w_branch, w_out, norm2_g, w_router, b_router, w_gate, b_gate, w_up, b_up, w_down, b_down):
    b, s, d = x.shape
    n = b * s
    depth = w_in.shape[0]
    q_lora = w_uq.shape[1]
    kv_lora = w_ukv.shape[1]
    heads = w_ukv.shape[2] // (NOPE_DIM + V_DIM)
    mem_len = mem.shape[1]
    assert heads * V_DIM == d and d % (2 * LANES) == 0
    lat = q_lora + kv_lora
    kr_end = lat + ROPE_DIM
    assert w_in.shape[2] == kr_end + 7 * d

    pos = positions.reshape(n, 1).astype(I32)
    xf = x.reshape(n, d)
    mem_f = mem.reshape(b * mem_len, d)

    for l in range(depth):
        w_l = w_in[l]
        w_big = _window_cast(w_l, kr_end, 7 * d)
        w_small = jnp.concatenate(
            [w_l[:, :kr_end], w_l[:, lat + ROPE_DIM // 2:kr_end], w_l[:, lat:lat + ROPE_DIM // 2]], axis=1).astype(BF16)
        wq3 = w_uq[l].reshape(q_lora, heads, QK_DIM)
        wq = jnp.concatenate([wq3, wq3[..., NOPE_DIM + ROPE_DIM // 2:], wq3[..., NOPE_DIM:NOPE_DIM + ROPE_DIM // 2]],
                             axis=-1).reshape(q_lora, heads * HEAD_COLS).astype(BF16)
        wkv = w_ukv[l].astype(BF16)

        xn = _rmsnorm(xf, norm1_g[l])
        big = _matmul(xn, w_big, BF16, name="in_proj_wide")
        small = _matmul(xn, w_small, BF16, tn=w_small.shape[1], name="in_proj_latent")

        q, k, v = _mla_projections(small, pos, b, s, heads, q_lora, kv_lora, q_lat_g[l], kv_lat_g[l], wq, wkv,
                                   qn_nope_g[l], qn_rope_g[l], kn_nope_g[l], kn_rope_g[l])
        a_mla = _flash_attention(q, k, v)

        a_conv = _short_conv(big, conv_w[l], n, s, d)

        mem_n = _rmsnorm(mem_f, mem_norm_g[l])
        kvm = _matmul(mem_n, w_mem_kv[l].astype(BF16), BF16, name="mem_kv_proj")
        a_mem = _memory_attention(big, kvm, mem_qn_g[l], mem_kn_g[l], b, s, d, mem_len, q_seg=3)

        merged = _gated_merge(a_mla, a_conv, a_mem, w_branch[l].astype(BF16), big, gate_seg=4)
        x1 = _matmul(merged, w_out[l].astype(BF16), F32, res=xf, name="out_proj")

        xf = _routed_experts(x1, norm2_g[l], w_router[l], b_router[l],
                             w_gate[l], b_gate[l], w_up[l], b_up[l], w_down[l], b_down[l])
    return xf.reshape(b, s, d)
```

```python
import functools
import math

import jax
import jax.numpy as jnp
from jax import lax
from jax.experimental import pallas as pl
from jax.experimental.pallas import tpu as pltpu

F32 = jnp.float32
BF16 = jnp.bfloat16
U32 = jnp.uint32
I32 = jnp.int32

EPS = 1e-6
NOPE_DIM = 128
ROPE_DIM = 64
V_DIM = 128
QK_DIM = NOPE_DIM + ROPE_DIM
HEAD_COLS = 256
MEM_HEADS = 4
TOP_K = 4
ROPE_THETA = 10000.0
SWIGLU_LIMIT = 7.0
SWIGLU_ALPHA = 1.702
LANES = 128
V7X_VMEM_LIMIT_BYTES = 56 * 1024 * 1024
HI_MASK = 0xFFFF0000
LOG2_E = 1.4426950408889634


def _params(*sem):
    return pltpu.CompilerParams(dimension_semantics=sem, vmem_limit_bytes=V7X_VMEM_LIMIT_BYTES)


def _tile(n, want):
    t = min(n, want)
    while n % t:
        t //= 2
    assert t >= 1
    return t


def _bits(x):
    return lax.bitcast_convert_type(x, U32)


def _pack_pair(lo, hi):
    lo_bits = _bits(lo.astype(BF16).astype(F32)) >> 16
    hi_bits = _bits(hi.astype(BF16).astype(F32)) & jnp.uint32(HI_MASK)
    return lo_bits | hi_bits


def _pack_halves(y):
    half = y.shape[1] // 2
    return _pack_pair(y[:, :half], y[:, half:])


def _unpack_halves(p):
    lo = lax.bitcast_convert_type(p << 16, F32)
    hi = lax.bitcast_convert_type(p & jnp.uint32(HI_MASK), F32)
    return lo, hi


def _rmsnorm_kernel(x_ref, g_ref, o_ref):
    x = x_ref[...].astype(F32)
    r = lax.rsqrt(jnp.mean(x * x, axis=-1, keepdims=True) + EPS)
    o_ref[...] = (x * r * g_ref[...]).astype(o_ref.dtype)


def _rmsnorm(x, g, tm=512):
    m, d = x.shape
    tm = _tile(m, tm)
    return pl.pallas_call(
        _rmsnorm_kernel,
        grid=(m // tm,),
        in_specs=[pl.BlockSpec((tm, d), lambda i: (i, 0)), pl.BlockSpec((1, d), lambda i: (0, 0))],
        out_specs=pl.BlockSpec((tm, d), lambda i: (i, 0)),
        out_shape=jax.ShapeDtypeStruct((m, d), BF16),
        compiler_params=_params("parallel"),
        name="rmsnorm",
    )(x, g.reshape(1, d))


def _mm_kernel(x_ref, w_ref, o_ref):
    o_ref[...] = jnp.dot(x_ref[...], w_ref[...], preferred_element_type=F32).astype(o_ref.dtype)


def _mm_nt_kernel(x_ref, wt_ref, o_ref):
    o_ref[...] = lax.dot_general(x_ref[...], wt_ref[...], (((1,), (1,)), ((), ())),
                                 preferred_element_type=F32).astype(o_ref.dtype)


def _matmul_nt(x, wt, out_dtype, row0=0, n=None, tm=1024, tn=1024, name="matmul_nt"):
    m, k = x.shape
    n = wt.shape[0] - row0 if n is None else n
    tm, tn = _tile(m, tm), _tile(n, tn)
    return pl.pallas_call(
        _mm_nt_kernel,
        grid=(m // tm, n // tn),
        in_specs=[pl.BlockSpec((tm, k), lambda i, j: (i, 0)),
                  pl.BlockSpec((pl.Element(tn), pl.Element(k)),
                               lambda i, j: (pl.multiple_of(row0 + j * tn, math.gcd(row0, tn)), 0))],
        out_specs=pl.BlockSpec((tm, tn), lambda i, j: (i, j)),
        out_shape=jax.ShapeDtypeStruct((m, n), out_dtype),
        compiler_params=_params("parallel", "parallel"),
        name=name,
    )(x, wt)


def _mm_res_kernel(x_ref, w_ref, r_ref, o_ref):
    o_ref[...] = r_ref[...] + jnp.dot(x_ref[...], w_ref[...], preferred_element_type=F32)


def _matmul(x, w, out_dtype, tm=1024, tn=1024, res=None, name="matmul"):
    m, k = x.shape
    n = w.shape[1]
    tm, tn = _tile(m, tm), _tile(n, tn)
    in_specs = [pl.BlockSpec((tm, k), lambda i, j: (i, 0)), pl.BlockSpec((k, tn), lambda i, j: (0, j))]
    args = [x, w]
    body = _mm_kernel
    if res is not None:
        in_specs.append(pl.BlockSpec((tm, tn), lambda i, j: (i, j)))
        args.append(res)
        body = _mm_res_kernel
    return pl.pallas_call(
        body,
        grid=(m // tm, n // tn),
        in_specs=in_specs,
        out_specs=pl.BlockSpec((tm, tn), lambda i, j: (i, j)),
        out_shape=jax.ShapeDtypeStruct((m, n), out_dtype),
        compiler_params=_params("parallel", "parallel"),
        name=name,
    )(*args)


def _rope_tables(pos_ref, invf_ref, sgn_ref, cos_s, sin_s):
    ang = pos_ref[...].astype(F32) * invf_ref[...]
    cos_s[...] = jnp.cos(ang)
    sin_s[...] = jnp.sin(ang) * sgn_ref[...]


def _rope_pair(slab, gain, cos, sin):
    r = lax.rsqrt(jnp.mean(slab * slab, axis=-1, keepdims=True) + EPS)
    slab = slab * r * gain
    return slab * cos + pltpu.roll(slab, ROPE_DIM, 1) * sin


def _qproj_kernel(cq_ref, pos_ref, invf_ref, sgn_ref, glat_ref, gq_ref, w_ref, o_ref, cqn_s, cos_s, sin_s, *, scale):
    @pl.when(pl.program_id(1) == 0)
    def _():
        c = cq_ref[...].astype(F32)
        r = lax.rsqrt(jnp.mean(c * c, axis=-1, keepdims=True) + EPS)
        cqn_s[...] = (c * r * glat_ref[...]).astype(BF16)
        _rope_tables(pos_ref, invf_ref, sgn_ref, cos_s, sin_s)

    for g in range(o_ref.shape[0]):
        q = jnp.dot(cqn_s[...], w_ref[:, g * HEAD_COLS:(g + 1) * HEAD_COLS], preferred_element_type=F32)
        nope = q[:, :NOPE_DIM]
        rn = lax.rsqrt(jnp.mean(nope * nope, axis=-1, keepdims=True) + EPS)
        nope = nope * rn * gq_ref[:, :NOPE_DIM]
        rot = _rope_pair(q[:, NOPE_DIM:], gq_ref[:, NOPE_DIM:], cos_s[...], sin_s[...])
        o_ref[g, :, :NOPE_DIM] = (nope * scale).astype(o_ref.dtype)
        o_ref[g, :, NOPE_DIM:] = (rot[:, :ROPE_DIM] * scale).astype(o_ref.dtype)


def _kvproj_kernel(ckv_ref, kr_ref, pos_ref, invf_ref, sgn_ref, glat_ref, gkr_ref, gkn_ref, w_ref,
                   k_ref, v_ref, ckvn_s, kr_s, cos_s, sin_s):
    @pl.when(pl.program_id(1) == 0)
    def _():
        c = ckv_ref[...].astype(F32)
        r = lax.rsqrt(jnp.mean(c * c, axis=-1, keepdims=True) + EPS)
        ckvn_s[...] = (c * r * glat_ref[...]).astype(BF16)
        _rope_tables(pos_ref, invf_ref, sgn_ref, cos_s, sin_s)
        kr_s[...] = _rope_pair(kr_ref[...].astype(F32), gkr_ref[...], cos_s[...], sin_s[...]).astype(BF16)

    for g in range(k_ref.shape[0]):
        kv = jnp.dot(ckvn_s[...], w_ref[:, g * HEAD_COLS:(g + 1) * HEAD_COLS], preferred_element_type=F32)
        kn = kv[:, :NOPE_DIM]
        rn = lax.rsqrt(jnp.mean(kn * kn, axis=-1, keepdims=True) + EPS)
        k_ref[g, :, :NOPE_DIM] = (kn * rn * gkn_ref[...]).astype(k_ref.dtype)
        k_ref[g, :, NOPE_DIM:] = kr_s[:, :ROPE_DIM]
        v_ref[g] = kv[:, NOPE_DIM:].astype(v_ref.dtype)


def _rope_consts():
    inv_freq = ROPE_THETA ** (-jnp.arange(0, ROPE_DIM, 2, dtype=F32) / ROPE_DIM)
    invf = jnp.tile(inv_freq, LANES // (ROPE_DIM // 2)).reshape(1, LANES)
    half = jnp.concatenate([-jnp.ones((ROPE_DIM // 2,), F32), jnp.ones((ROPE_DIM // 2,), F32)])
    sgn = jnp.tile(half, LANES // ROPE_DIM).reshape(1, LANES)
    return invf, sgn


def _swap_halves(g):
    return jnp.concatenate([g[..., ROPE_DIM // 2:], g[..., :ROPE_DIM // 2]], axis=-1)


def _mla_projections(small, pos, b, s, heads, q_lora, kv_lora, q_lat_g, kv_lat_g, wq, wkv,
                     qn_nope_g, qn_rope_g, kn_nope_g, kn_rope_g, tm=512, hps=8):
    n = b * s
    tm = _tile(s, tm)
    spb = s // tm
    invf, sgn = _rope_consts()
    scale = float(QK_DIM) ** -0.5 * LOG2_E
    gq = jnp.concatenate([qn_nope_g, qn_rope_g, _swap_halves(qn_rope_g)]).reshape(1, HEAD_COLS)
    gkr = jnp.concatenate([kn_rope_g, _swap_halves(kn_rope_g)]).reshape(1, LANES)
    hps = _tile(heads, hps)
    row = lambda i, h: (i, 0)
    const = lambda i, h: (0, 0)
    head_out = lambda i, h: (i // spb, h, i % spb, 0)
    assert q_lora % kv_lora == 0 and kv_lora % LANES == 0
    ckv_blk = q_lora // kv_lora
    kr_blk = (q_lora + kv_lora) // LANES

    q = pl.pallas_call(
        functools.partial(_qproj_kernel, scale=scale),
        grid=(n // tm, heads // hps),
        in_specs=[pl.BlockSpec((tm, q_lora), row), pl.BlockSpec((tm, 1), row),
                  pl.BlockSpec((1, LANES), const), pl.BlockSpec((1, LANES), const),
                  pl.BlockSpec((1, q_lora), const), pl.BlockSpec((1, HEAD_COLS), const),
                  pl.BlockSpec((q_lora, hps * HEAD_COLS), lambda i, h: (0, h))],
        out_specs=pl.BlockSpec((None, hps, tm, QK_DIM), head_out),
        out_shape=jax.ShapeDtypeStruct((b, heads, s, QK_DIM), BF16),
        scratch_shapes=[pltpu.VMEM((tm, q_lora), BF16), pltpu.VMEM((tm, LANES), F32), pltpu.VMEM((tm, LANES), F32)],
        compiler_params=_params("parallel", "arbitrary"),
        name="mla_q_proj",
    )(small, pos, invf, sgn, q_lat_g.reshape(1, q_lora), gq, wq)

    k, v = pl.pallas_call(
        _kvproj_kernel,
        grid=(n // tm, heads // hps),
        in_specs=[pl.BlockSpec((tm, kv_lora), lambda i, h: (i, ckv_blk)),
                  pl.BlockSpec((tm, LANES), lambda i, h: (i, kr_blk)),
                  pl.BlockSpec((tm, 1), row),
                  pl.BlockSpec((1, LANES), const), pl.BlockSpec((1, LANES), const),
                  pl.BlockSpec((1, kv_lora), const), pl.BlockSpec((1, LANES), const),
                  pl.BlockSpec((1, NOPE_DIM), const),
                  pl.BlockSpec((kv_lora, hps * HEAD_COLS), lambda i, h: (0, h))],
        out_specs=[pl.BlockSpec((None, hps, tm, QK_DIM), head_out),
                   pl.BlockSpec((None, hps, tm, V_DIM), head_out)],
        out_shape=[jax.ShapeDtypeStruct((b, heads, s, QK_DIM), BF16),
                   jax.ShapeDtypeStruct((b, heads, s, V_DIM), BF16)],
        scratch_shapes=[pltpu.VMEM((tm, kv_lora), BF16), pltpu.VMEM((tm, LANES), BF16),
                        pltpu.VMEM((tm, LANES), F32), pltpu.VMEM((tm, LANES), F32)],
        compiler_params=_params("parallel", "arbitrary"),
        name="mla_kv_proj",
    )(small, small, pos, invf, sgn, kv_lat_g.reshape(1, kv_lora), gkr, kn_nope_g.reshape(1, NOPE_DIM), wkv)
    return q, k, v


def _flash_kernel(q_ref, k_ref, v_ref, o_ref, s_a, s_b, m_s, l_s, acc_s, *, tq, tk):
    n_q = q_ref.shape[0] // tq

    def scores(j, dst, q0, r0=0):
        ks = k_ref[pl.ds(pl.multiple_of(j * tk, tk), tk), :]
        dst[r0:, :] = lax.dot_general(q_ref[q0 + r0:q0 + tq, :], ks, (((1,), (1,)), ((), ())),
                                      preferred_element_type=F32)

    def soft_pv(src, j, q0, r0=0, diagonal=False):
        rows = slice(q0 + r0, q0 + tq)
        s = src[r0:, :]
        if diagonal:
            row = lax.broadcasted_iota(I32, s.shape, 0)
            col = lax.broadcasted_iota(I32, s.shape, 1)
            s = jnp.where(col <= row, s, -jnp.inf)
        m_prev = m_s[rows, :]
        m_new = jnp.maximum(m_prev, jnp.max(s, axis=-1, keepdims=True))
        alpha = jnp.exp2(m_prev - m_new)
        ps = [jnp.exp2(s[:, c * LANES:(c + 1) * LANES] - m_new) for c in range(tk // LANES)]
        part = ps[0]
        for x in ps[1:]:
            part = part + x
        l_s[rows, :] = alpha * l_s[rows, :] + part
        m_s[rows, :] = m_new
        vs = v_ref[pl.ds(pl.multiple_of(j * tk, tk), tk), :]
        p = jnp.concatenate(ps, axis=1).astype(BF16)
        acc_s[rows, :] = alpha * acc_s[rows, :] + jnp.dot(p, vs, preferred_element_type=F32)

    m_s[...] = jnp.full_like(m_s, -jnp.inf)
    l_s[...] = jnp.zeros_like(l_s)
    acc_s[...] = jnp.zeros_like(acc_s)

    for i in range(n_q):
        q0 = i * tq
        scores(0, s_a, q0)

        def pair(jj, c, q0=q0):
            j = 2 * jj
            scores(j + 1, s_b, q0)
            soft_pv(s_a, j, q0)
            scores(j + 2, s_a, q0)
            soft_pv(s_b, j + 1, q0)
            return c

        if i:
            lax.fori_loop(0, i, pair, 0)
        scores(2 * i + 1, s_b, q0, tk)
        soft_pv(s_a, 2 * i, q0, 0, True)
        soft_pv(s_b, 2 * i + 1, q0, tk, True)
        l = jnp.sum(l_s[q0:q0 + tq, :], axis=-1, keepdims=True)
        o_ref[q0:q0 + tq, :] = (acc_s[q0:q0 + tq, :] / l).astype(o_ref.dtype)


def _flash_attention(q, k, v, tq=1024):
    b, heads, s, _ = q.shape
    tq = _tile(s, tq)
    tk = tq // 2
    assert tk % LANES == 0
    whole = lambda bi, h: (bi, h, 0, 0)
    return pl.pallas_call(
        functools.partial(_flash_kernel, tq=tq, tk=tk),
        grid=(b, heads),
        in_specs=[pl.BlockSpec((None, None, s, QK_DIM), whole), pl.BlockSpec((None, None, s, QK_DIM), whole),
                  pl.BlockSpec((None, None, s, V_DIM), whole)],
        out_specs=pl.BlockSpec((s, V_DIM), lambda bi, h: (bi, h)),
        out_shape=jax.ShapeDtypeStruct((b * s, heads * V_DIM), BF16),
        scratch_shapes=[pltpu.VMEM((tq, tk), F32), pltpu.VMEM((tq, tk), F32),
                        pltpu.VMEM((s, LANES), F32), pltpu.VMEM((s, LANES), F32), pltpu.VMEM((s, V_DIM), F32)],
        compiler_params=_params("parallel", "parallel"),
        name="mla_flash_attention",
    )(q, k, v)


def _conv_kernel(cb_ref, cc_ref, ch_ref, pc_ref, ph_ref, w_ref, o_ref, *, tiles_per_seq):
    first = (pl.program_id(0) % tiles_per_seq) == 0
    u = cc_ref[...].astype(F32) * ch_ref[...].astype(F32)
    halo = pc_ref[...].astype(F32) * ph_ref[...].astype(F32)
    halo = jnp.where(first, 0.0, halo)
    cat = jnp.concatenate([halo, u], axis=0)
    n_halo = halo.shape[0]
    u1 = pltpu.roll(cat, 1, 0)[n_halo:]
    u2 = pltpu.roll(cat, 2, 0)[n_halo:]
    y = w_ref[0:1, :] * u2 + w_ref[1:2, :] * u1 + w_ref[2:3, :] * u
    o_ref[...] = (cb_ref[...].astype(F32) * y).astype(o_ref.dtype)


def _short_conv(big, conv_w, n, s, d, ts=512, tc=1024):
    ts, tc = _tile(s, ts), _tile(d, tc)
    halo = 8
    cpb = d // tc
    seg = lambda k: (lambda i, j: (i, k * cpb + j))
    prev = lambda k: (lambda i, j: (jnp.maximum(i * (ts // halo) - 1, 0), k * cpb + j))
    return pl.pallas_call(
        functools.partial(_conv_kernel, tiles_per_seq=s // ts),
        grid=(n // ts, cpb),
        in_specs=[pl.BlockSpec((ts, tc), seg(0)), pl.BlockSpec((ts, tc), seg(1)), pl.BlockSpec((ts, tc), seg(2)),
                  pl.BlockSpec((halo, tc), prev(1)), pl.BlockSpec((halo, tc), prev(2)),
                  pl.BlockSpec((conv_w.shape[0], tc), lambda i, j: (0, j))],
        out_specs=pl.BlockSpec((ts, tc), lambda i, j: (i, j)),
        out_shape=jax.ShapeDtypeStruct((n, d), BF16),
        compiler_params=_params("parallel", "parallel"),
        name="short_conv",
    )(big, big, big, big, big, conv_w)


def _memattn_kernel(q_ref, k_ref, v_ref, gq_ref, gk_ref, o_ref, kn_s, *, scale):
    @pl.when(pl.program_id(2) == 0)
    def _():
        k = k_ref[...].astype(F32)
        r = lax.rsqrt(jnp.mean(k * k, axis=-1, keepdims=True) + EPS)
        kn_s[...] = (k * r * gk_ref[...]).astype(BF16)

    q = q_ref[...].astype(F32)
    r = lax.rsqrt(jnp.mean(q * q, axis=-1, keepdims=True) + EPS)
    qn = (q * (r * scale) * gq_ref[...]).astype(BF16)
    s = lax.dot_general(qn, kn_s[...], (((1,), (1,)), ((), ())), preferred_element_type=F32)
    m = jnp.max(s, axis=-1, keepdims=True)
    p = jnp.exp(s - m)
    l = jnp.sum(p, axis=-1, keepdims=True)
    o = jnp.dot(p.astype(BF16), v_ref[...], preferred_element_type=F32)
    o_ref[...] = (o / l).astype(o_ref.dtype)


def _memory_attention(big, kvm, gq, gk, b, s, d, mem_len, q_seg, ts=512):
    hd = d // MEM_HEADS
    ts = _tile(s, ts)
    spb = s // ts
    return pl.pallas_call(
        functools.partial(_memattn_kernel, scale=float(hd) ** -0.5),
        grid=(b, MEM_HEADS, spb),
        in_specs=[pl.BlockSpec((ts, hd), lambda bi, h, i: (bi * spb + i, q_seg * MEM_HEADS + h)),
                  pl.BlockSpec((mem_len, hd), lambda bi, h, i: (bi, h)),
                  pl.BlockSpec((mem_len, hd), lambda bi, h, i: (bi, MEM_HEADS + h)),
                  pl.BlockSpec((1, hd), lambda bi, h, i: (0, 0)),
                  pl.BlockSpec((1, hd), lambda bi, h, i: (0, 0))],
        out_specs=pl.BlockSpec((ts, hd), lambda bi, h, i: (bi * spb + i, h)),
        out_shape=jax.ShapeDtypeStruct((b * s, d), BF16),
        scratch_shapes=[pltpu.VMEM((mem_len, hd), BF16)],
        compiler_params=_params("parallel", "parallel", "arbitrary"),
        name="memory_attention",
    )(big, kvm, kvm, gq.reshape(1, hd), gk.reshape(1, hd))


def _merge_kernel(a0_ref, a1_ref, a2_ref, w_ref, g_ref, o_ref, acc_s):
    br = pl.program_id(2)
    gate = jax.nn.sigmoid(g_ref[...].astype(F32))

    @pl.when(br == 0)
    def _():
        acc_s[...] = gate * jnp.dot(a0_ref[...], w_ref[...], preferred_element_type=F32)

    @pl.when(br == 1)
    def _():
        acc_s[...] += gate * jnp.dot(a1_ref[...], w_ref[...], preferred_element_type=F32)

    @pl.when(br == 2)
    def _():
        o_ref[...] = (acc_s[...] + gate * jnp.dot(a2_ref[...], w_ref[...], preferred_element_type=F32)).astype(o_ref.dtype)


def _gated_merge(a_mla, a_conv, a_mem, w_branch, big, gate_seg, tm=512, tn=1024):
    n, d = a_mla.shape
    tm, tn = _tile(n, tm), _tile(d, tn)
    cpb = d // tn
    lhs = pl.BlockSpec((tm, d), lambda i, j, br: (i, 0))
    return pl.pallas_call(
        _merge_kernel,
        grid=(n // tm, cpb, 3),
        in_specs=[lhs, lhs, lhs,
                  pl.BlockSpec((None, d, tn), lambda i, j, br: (br, 0, j)),
                  pl.BlockSpec((tm, tn), lambda i, j, br: (i, (gate_seg + br) * cpb + j))],
        out_specs=pl.BlockSpec((tm, tn), lambda i, j, br: (i, j)),
        out_shape=jax.ShapeDtypeStruct((n, d), BF16),
        scratch_shapes=[pltpu.VMEM((tm, tn), F32)],
        compiler_params=_params("parallel", "parallel", "arbitrary"),
        name="gated_merge",
    )(a_mla, a_conv, a_mem, w_branch, big)


def _router_kernel(x_ref, g_ref, wr_ref, br_ref, hp_ref, idx_ref, wt_ref, rank_ref, cnt_ref, carry_s, *, n_exp):
    tm = x_ref.shape[0]

    @pl.when(pl.program_id(0) == 0)
    def _():
        carry_s[...] = jnp.zeros_like(carry_s)

    x = x_ref[...]
    r = lax.rsqrt(jnp.mean(x * x, axis=-1, keepdims=True) + EPS)
    h = x * r * g_ref[...]
    hp_ref[...] = _pack_halves(h)

    logits = jnp.dot(h, wr_ref[...], preferred_element_type=F32, precision=lax.Precision.HIGHEST) + br_ref[...]
    lane = lax.broadcasted_iota(I32, (tm, n_exp), 1)
    out_lane = lax.broadcasted_iota(I32, (tm, LANES), 1)
    work = logits
    vals, hots = [], []
    idx_out = jnp.zeros((tm, LANES), I32)
    for k in range(TOP_K):
        m = jnp.max(work, axis=-1, keepdims=True)
        idx = jnp.min(jnp.where(work == m, lane, n_exp), axis=-1, keepdims=True)
        hot = lane == idx
        vals.append(m)
        hots.append(hot)
        idx_out = jnp.where(out_lane == k, idx, idx_out)
        work = jnp.where(hot, -jnp.inf, work)
    idx_ref[...] = idx_out

    exps = [jnp.exp(v - vals[0]) for v in vals]
    denom = exps[0]
    for e in exps[1:]:
        denom = denom + e
    wt_out = jnp.zeros((tm, LANES), F32)
    for k in range(TOP_K):
        wt_out = jnp.where(out_lane == k, exps[k] / denom, wt_out)
    wt_ref[...] = wt_out

    member = jnp.zeros((tm, n_exp), F32)
    for hot in hots:
        member = member + hot.astype(F32)
    tri = (lax.broadcasted_iota(I32, (tm, tm), 1) < lax.broadcasted_iota(I32, (tm, tm), 0)).astype(BF16)
    prefix = jnp.dot(tri, member.astype(BF16), preferred_element_type=F32) + carry_s[0:1, 0:n_exp]
    rank_out = jnp.zeros((tm, LANES), I32)
    for k in range(TOP_K):
        rk = jnp.sum(jnp.where(hots[k], prefix, 0.0), axis=-1, keepdims=True).astype(I32)
        rank_out = jnp.where(out_lane == k, rk, rank_out)
    rank_ref[...] = rank_out

    carry_s[0:1, 0:n_exp] = carry_s[0:1, 0:n_exp] + jnp.sum(member, axis=0, keepdims=True)
    cnt_ref[...] = carry_s[...]


def _router(x1, g, w_router, b_router, tm=512):
    n, d = x1.shape
    n_exp = w_router.shape[1]
    assert n_exp <= LANES
    tm = _tile(n, tm)
    row = lambda i: (i, 0)
    const = lambda i: (0, 0)
    return pl.pallas_call(
        functools.partial(_router_kernel, n_exp=n_exp),
        grid=(n // tm,),
        in_specs=[pl.BlockSpec((tm, d), row), pl.BlockSpec((1, d), const),
                  pl.BlockSpec((d, n_exp), const), pl.BlockSpec((1, n_exp), const)],
        out_specs=[pl.BlockSpec((tm, d // 2), row), pl.BlockSpec((tm, LANES), row), pl.BlockSpec((tm, LANES), row),
                   pl.BlockSpec((tm, LANES), row), pl.BlockSpec((8, LANES), const)],
        out_shape=[jax.ShapeDtypeStruct((n, d // 2), U32), jax.ShapeDtypeStruct((n, LANES), I32),
                   jax.ShapeDtypeStruct((n, LANES), F32), jax.ShapeDtypeStruct((n, LANES), I32),
                   jax.ShapeDtypeStruct((8, LANES), F32)],
        scratch_shapes=[pltpu.VMEM((8, LANES), F32)],
        compiler_params=_params("arbitrary"),
        name="router",
    )(x1, g.reshape(1, d), w_router, b_router.reshape(1, n_exp))


def _dispatch_kernel(dest_ref, zflag_ref, hp_ref, xs_ref, zeros_s, sem, zsem, *, rows):
    base = pl.program_id(0) * rows
    te = zeros_s.shape[0]

    @pl.when(pl.program_id(0) == 0)
    def _():
        zeros_s[...] = jnp.zeros_like(zeros_s)

        def fill(t):
            return pltpu.make_async_copy(zeros_s, xs_ref.at[pl.ds(pl.multiple_of(t * te, te), te)], zsem)

        def start(t, c):
            @pl.when(zflag_ref[t] != 0)
            def _():
                fill(t).start()
            return c

        def finish(t, c):
            @pl.when(zflag_ref[t] != 0)
            def _():
                fill(t).wait()
            return c

        lax.fori_loop(0, zflag_ref.shape[0], start, 0)
        lax.fori_loop(0, zflag_ref.shape[0], finish, 0)

    def issue(r, c):
        for k in range(TOP_K):
            d = dest_ref[(base + r) * TOP_K + k]
            pltpu.make_async_copy(hp_ref.at[pl.ds(r, 1)], xs_ref.at[pl.ds(d, 1)], sem).start()
        return c

    lax.fori_loop(0, rows, issue, 0)

    def drain(r, c):
        for k in range(TOP_K):
            pltpu.make_async_copy(hp_ref.at[pl.ds(0, 1)], xs_ref.at[pl.ds(0, 1)], sem).wait()
        return c

    lax.fori_loop(0, rows, drain, 0)


def _dispatch(dest_flat, zero_flag, hp, te, rows=256):
    n, half = hp.shape
    rows = _tile(n, rows)
    n_slots = zero_flag.shape[0] * te
    return pl.pallas_call(
        functools.partial(_dispatch_kernel, rows=rows),
        grid_spec=pltpu.PrefetchScalarGridSpec(
            num_scalar_prefetch=2,
            grid=(n // rows,),
            in_specs=[pl.BlockSpec((rows, half), lambda i, dest, zflag: (i, 0))],
            out_specs=pl.BlockSpec(memory_space=pl.ANY),
            scratch_shapes=[pltpu.VMEM((te, half), U32), pltpu.SemaphoreType.DMA(()), pltpu.SemaphoreType.DMA(())],
        ),
        out_shape=jax.ShapeDtypeStruct((n_slots, half), U32),
        compiler_params=_params("arbitrary"),
        name="expert_dispatch",
    )(dest_flat, zero_flag, hp)


def _group_items(tile_e, first_tile, tiles_of, rows_of, n_used, n_blocks, n_tiles, te):
    n_items = n_used * n_blocks
    w = jnp.arange(n_tiles * n_blocks, dtype=I32)
    wc = jnp.minimum(w, n_items - 1)
    e = tile_e[wc // n_blocks]
    first, count = first_tile[e], jnp.maximum(tiles_of[e], 1)
    r = wc - first * n_blocks
    in_tile, in_blk = first + r % count, r // count
    z = jnp.maximum(w - n_items, 0)
    live = w < n_items
    out_tile = jnp.where(live, in_tile, n_used + z // n_blocks)
    out_blk = jnp.where(live, in_blk, z % n_blocks)
    starts = (live & (r % count == 0)).astype(I32)
    short = (rows_of[e] - (in_tile - first) * te <= te // 2).astype(I32)
    return in_tile, in_blk, e, starts, short, out_tile, out_blk, n_items.reshape(1)


def _expert_up_kernel(tile_ref, blk_ref, exp_ref, start_ref, short_ref, ot_ref, ob_ref, ni_ref,
                      xs_ref, wg_ref, bg_ref, wu_ref, bu_ref, act_ref, xb_s, wg_s, wu_s):
    w = pl.program_id(0)
    live = w < ni_ref[0]
    te, half = xs_ref.shape

    @pl.when(start_ref[w] != 0)
    def _():
        wg_s[...] = wg_ref[...].astype(BF16)
        wu_s[...] = wu_ref[...].astype(BF16)

    def compute(rows):
        lo, hi = _unpack_halves(xs_ref[:rows, :])
        xb_s[:rows, :half] = lo.astype(BF16)
        xb_s[:rows, half:] = hi.astype(BF16)
        xb = xb_s[:rows, :]
        g = jnp.dot(xb, wg_s[...], preferred_element_type=F32) + bg_ref[...]
        u = jnp.dot(xb, wu_s[...], preferred_element_type=F32) + bu_ref[...]
        g = jnp.minimum(g, SWIGLU_LIMIT)
        u = jnp.clip(u, -SWIGLU_LIMIT, SWIGLU_LIMIT)
        act_ref[:rows, :] = (g * jax.nn.sigmoid(SWIGLU_ALPHA * g) * (u + 1.0)).astype(act_ref.dtype)
        if rows < te:
            act_ref[rows:, :] = jnp.zeros((te - rows, act_ref.shape[1]), act_ref.dtype)

    @pl.when(jnp.logical_and(live, short_ref[w] == 0))
    def _():
        compute(te)

    @pl.when(jnp.logical_and(live, short_ref[w] != 0))
    def _():
        compute(te // 2)

    @pl.when(jnp.logical_not(live))
    def _():
        act_ref[...] = jnp.zeros_like(act_ref)


def _expert_down_kernel(tile_ref, blk_ref, exp_ref, start_ref, short_ref, ot_ref, ob_ref, ni_ref,
                        act_ref, wdl_ref, wdh_ref, bdl_ref, bdh_ref, o_ref, wl_s, wh_s):
    w = pl.program_id(0)
    live = w < ni_ref[0]
    te = act_ref.shape[0]

    @pl.when(start_ref[w] != 0)
    def _():
        wl_s[...] = wdl_ref[...].astype(BF16)
        wh_s[...] = wdh_ref[...].astype(BF16)

    def compute(rows):
        a = act_ref[:rows, :]
        lo = jnp.dot(a, wl_s[...], preferred_element_type=F32) + bdl_ref[...]
        hi = jnp.dot(a, wh_s[...], preferred_element_type=F32) + bdh_ref[...]
        o_ref[:rows, :] = _pack_pair(lo, hi)
        if rows < te:
            o_ref[rows:, :] = jnp.zeros((te - rows, o_ref.shape[1]), o_ref.dtype)

    @pl.when(jnp.logical_and(live, short_ref[w] == 0))
    def _():
        compute(te)

    @pl.when(jnp.logical_and(live, short_ref[w] != 0))
    def _():
        compute(te // 2)

    @pl.when(jnp.logical_not(live))
    def _():
        o_ref[...] = jnp.zeros_like(o_ref)


def _expert_mlp(tile_e, first_tile, tiles_of, rows_of, n_used, xs, wg, bg, wu, bu, wd, bd, te, tcu=256, tnd=1024):
    n_slots, half = xs.shape
    d = 2 * half
    n_exp, _, de = wg.shape
    tcu, tnd = _tile(de, tcu), _tile(half, tnd)
    n_up, n_down = de // tcu, half // tnd
    n_tiles = n_slots // te
    n_pre = 8

    in_row = lambda w, tl, bk, ex, st, sh, ot, ob, ni: (tl[w], 0)
    w_blk = lambda w, tl, bk, ex, st, sh, ot, ob, ni: (ex[w], 0, bk[w])
    out_blk = lambda w, tl, bk, ex, st, sh, ot, ob, ni: (ot[w], ob[w])

    up_items = _group_items(tile_e, first_tile, tiles_of, rows_of, n_used, n_up, n_tiles, te)
    act = pl.pallas_call(
        _expert_up_kernel,
        grid_spec=pltpu.PrefetchScalarGridSpec(
            num_scalar_prefetch=n_pre,
            grid=(n_tiles * n_up,),
            in_specs=[pl.BlockSpec((te, half), in_row),
                      pl.BlockSpec((None, d, tcu), w_blk), pl.BlockSpec((None, 1, tcu), w_blk),
                      pl.BlockSpec((None, d, tcu), w_blk), pl.BlockSpec((None, 1, tcu), w_blk)],
            out_specs=pl.BlockSpec((te, tcu), out_blk),
            scratch_shapes=[pltpu.VMEM((te, d), BF16), pltpu.VMEM((d, tcu), BF16), pltpu.VMEM((d, tcu), BF16)],
        ),
        out_shape=jax.ShapeDtypeStruct((n_slots, de), BF16),
        compiler_params=_params("arbitrary"),
        name="expert_gate_up",
    )(*up_items, xs, wg, bg.reshape(n_exp, 1, de), wu, bu.reshape(n_exp, 1, de))

    hi_blk = lambda w, tl, bk, ex, st, sh, ot, ob, ni: (ex[w], 0, n_down + bk[w])
    bd3 = bd.reshape(n_exp, 1, d)
    down_items = _group_items(tile_e, first_tile, tiles_of, rows_of, n_used, n_down, n_tiles, te)
    return pl.pallas_call(
        _expert_down_kernel,
        grid_spec=pltpu.PrefetchScalarGridSpec(
            num_scalar_prefetch=n_pre,
            grid=(n_tiles * n_down,),
            in_specs=[pl.BlockSpec((te, de), in_row),
                      pl.BlockSpec((None, de, tnd), w_blk), pl.BlockSpec((None, de, tnd), hi_blk),
                      pl.BlockSpec((None, 1, tnd), w_blk), pl.BlockSpec((None, 1, tnd), hi_blk)],
            out_specs=pl.BlockSpec((te, tnd), out_blk),
            scratch_shapes=[pltpu.VMEM((de, tnd), BF16), pltpu.VMEM((de, tnd), BF16)],
        ),
        out_shape=jax.ShapeDtypeStruct((n_slots, half), U32),
        compiler_params=_params("arbitrary"),
        name="expert_down",
    )(*down_items, act, wd, wd, bd3, bd3)


def _combine_kernel(dest_ref, ys_ref, x_ref, wt_ref, o_ref, gbuf, sem, *, rows):
    base = pl.program_id(0) * rows
    half = gbuf.shape[2]

    def issue(r, c):
        for k in range(TOP_K):
            d = dest_ref[(base + r) * TOP_K + k]
            pltpu.make_async_copy(ys_ref.at[pl.ds(d, 1)], gbuf.at[k, pl.ds(r, 1)], sem).start()
        return c

    lax.fori_loop(0, rows, issue, 0)

    def drain(r, c):
        for k in range(TOP_K):
            pltpu.make_async_copy(ys_ref.at[pl.ds(0, 1)], gbuf.at[k, pl.ds(0, 1)], sem).wait()
        return c

    lax.fori_loop(0, rows, drain, 0)

    lo_sum = x_ref[:, :half]
    hi_sum = x_ref[:, half:]
    for k in range(TOP_K):
        lo, hi = _unpack_halves(gbuf[k])
        wk = wt_ref[:, k:k + 1]
        lo_sum = lo_sum + wk * lo
        hi_sum = hi_sum + wk * hi
    o_ref[:, :half] = lo_sum
    o_ref[:, half:] = hi_sum


def _combine(dest_flat, ys, x1, wt, rows=256):
    n, d = x1.shape
    rows = _tile(n, rows)
    return pl.pallas_call(
        functools.partial(_combine_kernel, rows=rows),
        grid_spec=pltpu.PrefetchScalarGridSpec(
            num_scalar_prefetch=1,
            grid=(n // rows,),
            in_specs=[pl.BlockSpec(memory_space=pl.ANY),
                      pl.BlockSpec((rows, d), lambda i, dest: (i, 0)),
                      pl.BlockSpec((rows, LANES), lambda i, dest: (i, 0))],
            out_specs=pl.BlockSpec((rows, d), lambda i, dest: (i, 0)),
            scratch_shapes=[pltpu.VMEM((TOP_K, rows, d // 2), U32), pltpu.SemaphoreType.DMA(())],
        ),
        out_shape=jax.ShapeDtypeStruct((n, d), F32),
        compiler_params=_params("arbitrary"),
        name="expert_combine",
    )(dest_flat, ys, x1, wt)


def _routed_experts(x1, norm_g, w_router, b_router, wg, bg, wu, bu, wd, bd, te=512):
    n, d = x1.shape
    n_exp = w_router.shape[1]
    hp, idx_p, wt_p, rank_p, cnt = _router(x1, norm_g, w_router, b_router)
    top_idx = idx_p[:, :TOP_K]
    rank = rank_p[:, :TOP_K]

    te = min(te, n * TOP_K)
    counts = cnt[0, :n_exp].astype(I32)
    padded = ((counts + te - 1) // te) * te
    pend = jnp.cumsum(padded)
    pstart = pend - padded
    n_tiles = (n * TOP_K) // te + n_exp
    n_used = (pend[-1] // te).astype(I32)
    tiles = jnp.arange(n_tiles, dtype=I32)
    tile_e = jnp.minimum(jnp.sum((pend[None, :] <= (tiles * te)[:, None]).astype(I32), axis=1), n_exp - 1)
    dest = (pstart[top_idx] + rank).reshape(n * TOP_K).astype(I32)
    partial_last = jnp.any((tiles[:, None] == (pend // te - 1)[None, :]) & ((counts % te) != 0)[None, :], axis=1)
    zero_flag = (partial_last | (tiles >= n_used)).astype(I32)

    xs = _dispatch(dest, zero_flag, hp, te)
    ys = _expert_mlp(tile_e, pstart // te, padded // te, counts, n_used, xs, wg, bg, wu, bu, wd, bd, te)
    return _combine(dest, ys, x1, wt_p)


def kernel(x, mem, positions, norm1_g, w_in, q_lat_g, kv_lat_g, w_uq, w_ukv, qn_nope_g, qn_rope_g, kn_nope_g, kn_rope_g, conv_w, mem_norm_g, w_mem_kv, mem_qn_g, mem_kn_g, w_branch, w_out, norm2_g, w_router, b_router, w_gate, b_gate, w_up, b_up, w_down, b_down):
    b, s, d = x.shape
    n = b * s
    depth = w_in.shape[0]
    q_lora = w_uq.shape[1]
    kv_lora = w_ukv.shape[1]
    heads = w_ukv.shape[2] // (NOPE_DIM + V_DIM)
    mem_len = mem.shape[1]
    assert heads * V_DIM == d and d % (2 * LANES) == 0
    lat = q_lora + kv_lora
    kr_end = lat + ROPE_DIM
    assert w_in.shape[2] == kr_end + 7 * d

    pos = positions.reshape(n, 1).astype(I32)
    xf = x.reshape(n, d)
    mem_f = mem.reshape(b * mem_len, d)

    for l in range(depth):
        w_l = w_in[l]
        w_t = jnp.swapaxes(w_l, 0, 1).astype(BF16)
        w_small_t = jnp.concatenate(
            [w_t[:kr_end], w_t[lat + ROPE_DIM // 2:kr_end], w_t[lat:lat + ROPE_DIM // 2]], axis=0)
        wq3 = w_uq[l].reshape(q_lora, heads, QK_DIM)
        wq = jnp.concatenate([wq3, wq3[..., NOPE_DIM + ROPE_DIM // 2:], wq3[..., NOPE_DIM:NOPE_DIM + ROPE_DIM // 2]],
                             axis=-1).reshape(q_lora, heads * HEAD_COLS).astype(BF16)
        wkv = w_ukv[l].astype(BF16)

        xn = _rmsnorm(xf, norm1_g[l])
        big = _matmul_nt(xn, w_t, BF16, row0=kr_end, n=7 * d, name="in_proj_wide")
        small = _matmul_nt(xn, w_small_t, BF16, tn=w_small_t.shape[0], name="in_proj_latent")

        q, k, v = _mla_projections(small, pos, b, s, heads, q_lora, kv_lora, q_lat_g[l], kv_lat_g[l], wq, wkv,
                                   qn_nope_g[l], qn_rope_g[l], kn_nope_g[l], kn_rope_g[l])
        a_mla = _flash_attention(q, k, v)

        a_conv = _short_conv(big, conv_w[l], n, s, d)

        mem_n = _rmsnorm(mem_f, mem_norm_g[l])
        kvm = _matmul(mem_n, w_mem_kv[l].astype(BF16), BF16, name="mem_kv_proj")
        a_mem = _memory_attention(big, kvm, mem_qn_g[l], mem_kn_g[l], b, s, d, mem_len, q_seg=3)

        merged = _gated_merge(a_mla, a_conv, a_mem, w_branch[l].astype(BF16), big, gate_seg=4)
        x1 = _matmul(merged, w_out[l].astype(BF16), F32, res=xf, name="out_proj")

        xf = _routed_experts(x1, norm2_g[l], w_router[l], b_router[l],
                             w_gate[l], b_gate[l], w_up[l], b_up[l], w_down[l], b_down[l])
    return xf.reshape(b, s, d)
```

```python
import functools
import math

import jax
import jax.numpy as jnp
from jax import lax
from jax.experimental import pallas as pl
from jax.experimental.pallas import tpu as pltpu

F32 = jnp.float32
BF16 = jnp.bfloat16
U32 = jnp.uint32
I32 = jnp.int32

EPS = 1e-6
NOPE_DIM = 128
ROPE_DIM = 64
V_DIM = 128
QK_DIM = NOPE_DIM + ROPE_DIM
HEAD_COLS = 256
MEM_HEADS = 4
TOP_K = 4
ROPE_THETA = 10000.0
SWIGLU_LIMIT = 7.0
SWIGLU_ALPHA = 1.702
LANES = 128
V7X_VMEM_LIMIT_BYTES = 56 * 1024 * 1024
HI_MASK = 0xFFFF0000
LOG2_E = 1.4426950408889634


def _params(*sem):
    return pltpu.CompilerParams(dimension_semantics=sem, vmem_limit_bytes=V7X_VMEM_LIMIT_BYTES)


def _tile(n, want):
    t = min(n, want)
    while n % t:
        t //= 2
    assert t >= 1
    return t


def _bits(x):
    return lax.bitcast_convert_type(x, U32)


def _pack_pair(lo, hi):
    lo_bits = _bits(lo.astype(BF16).astype(F32)) >> 16
    hi_bits = _bits(hi.astype(BF16).astype(F32)) & jnp.uint32(HI_MASK)
    return lo_bits | hi_bits


def _pack_halves(y):
    half = y.shape[1] // 2
    return _pack_pair(y[:, :half], y[:, half:])


def _unpack_halves(p):
    lo = lax.bitcast_convert_type(p << 16, F32)
    hi = lax.bitcast_convert_type(p & jnp.uint32(HI_MASK), F32)
    return lo, hi


def _rmsnorm_kernel(x_ref, g_ref, o_ref):
    x = x_ref[...].astype(F32)
    r = lax.rsqrt(jnp.mean(x * x, axis=-1, keepdims=True) + EPS)
    o_ref[...] = (x * r * g_ref[...]).astype(o_ref.dtype)


def _rmsnorm(x, g, tm=512):
    m, d = x.shape
    tm = _tile(m, tm)
    return pl.pallas_call(
        _rmsnorm_kernel,
        grid=(m // tm,),
        in_specs=[pl.BlockSpec((tm, d), lambda i: (i, 0)), pl.BlockSpec((1, d), lambda i: (0, 0))],
        out_specs=pl.BlockSpec((tm, d), lambda i: (i, 0)),
        out_shape=jax.ShapeDtypeStruct((m, d), BF16),
        compiler_params=_params("parallel"),
        name="rmsnorm",
    )(x, g.reshape(1, d))


def _mm_kernel(x_ref, w_ref, o_ref):
    o_ref[...] = jnp.dot(x_ref[...], w_ref[...], preferred_element_type=F32).astype(o_ref.dtype)


def _mm_nt_kernel(x_ref, wt_ref, o_ref):
    o_ref[...] = lax.dot_general(x_ref[...], wt_ref[...], (((1,), (1,)), ((), ())),
                                 preferred_element_type=F32).astype(o_ref.dtype)


def _matmul_nt(x, wt, out_dtype, row0=0, n=None, tm=1024, tn=1024, name="matmul_nt"):
    m, k = x.shape
    n = wt.shape[0] - row0 if n is None else n
    tm, tn = _tile(m, tm), _tile(n, tn)
    return pl.pallas_call(
        _mm_nt_kernel,
        grid=(m // tm, n // tn),
        in_specs=[pl.BlockSpec((tm, k), lambda i, j: (i, 0)),
                  pl.BlockSpec((pl.Element(tn), pl.Element(k)),
                               lambda i, j: (pl.multiple_of(row0 + j * tn, math.gcd(row0, tn)), 0))],
        out_specs=pl.BlockSpec((tm, tn), lambda i, j: (i, j)),
        out_shape=jax.ShapeDtypeStruct((m, n), out_dtype),
        compiler_params=_params("parallel", "parallel"),
        name=name,
    )(x, wt)


def _mm_res_kernel(x_ref, w_ref, r_ref, o_ref):
    o_ref[...] = r_ref[...] + jnp.dot(x_ref[...], w_ref[...], preferred_element_type=F32)


def _matmul(x, w, out_dtype, tm=1024, tn=1024, res=None, name="matmul"):
    m, k = x.shape
    n = w.shape[1]
    tm, tn = _tile(m, tm), _tile(n, tn)
    in_specs = [pl.BlockSpec((tm, k), lambda i, j: (i, 0)), pl.BlockSpec((k, tn), lambda i, j: (0, j))]
    args = [x, w]
    body = _mm_kernel
    if res is not None:
        in_specs.append(pl.BlockSpec((tm, tn), lambda i, j: (i, j)))
        args.append(res)
        body = _mm_res_kernel
    return pl.pallas_call(
        body,
        grid=(m // tm, n // tn),
        in_specs=in_specs,
        out_specs=pl.BlockSpec((tm, tn), lambda i, j: (i, j)),
        out_shape=jax.ShapeDtypeStruct((m, n), out_dtype),
        compiler_params=_params("parallel", "parallel"),
        name=name,
    )(*args)


def _rope_tables(pos_ref, invf_ref, sgn_ref, cos_s, sin_s):
    ang = pos_ref[...].astype(F32) * invf_ref[...]
    cos_s[...] = jnp.cos(ang)
    sin_s[...] = jnp.sin(ang) * sgn_ref[...]


def _rope_pair(slab, gain, cos, sin):
    r = lax.rsqrt(jnp.mean(slab * slab, axis=-1, keepdims=True) + EPS)
    slab = slab * r * gain
    return slab * cos + pltpu.roll(slab, ROPE_DIM, 1) * sin


def _qproj_kernel(cq_ref, pos_ref, invf_ref, sgn_ref, glat_ref, gq_ref, w_ref, o_ref, cqn_s, cos_s, sin_s, *, scale):
    @pl.when(pl.program_id(1) == 0)
    def _():
        c = cq_ref[...].astype(F32)
        r = lax.rsqrt(jnp.mean(c * c, axis=-1, keepdims=True) + EPS)
        cqn_s[...] = (c * r * glat_ref[...]).astype(BF16)
        _rope_tables(pos_ref, invf_ref, sgn_ref, cos_s, sin_s)

    for g in range(o_ref.shape[0]):
        q = jnp.dot(cqn_s[...], w_ref[:, g * HEAD_COLS:(g + 1) * HEAD_COLS], preferred_element_type=F32)
        nope = q[:, :NOPE_DIM]
        rn = lax.rsqrt(jnp.mean(nope * nope, axis=-1, keepdims=True) + EPS)
        nope = nope * rn * gq_ref[:, :NOPE_DIM]
        rot = _rope_pair(q[:, NOPE_DIM:], gq_ref[:, NOPE_DIM:], cos_s[...], sin_s[...])
        o_ref[g, :, :NOPE_DIM] = (nope * scale).astype(o_ref.dtype)
        o_ref[g, :, NOPE_DIM:] = (rot[:, :ROPE_DIM] * scale).astype(o_ref.dtype)


def _kvproj_kernel(ckv_ref, kr_ref, pos_ref, invf_ref, sgn_ref, glat_ref, gkr_ref, gkn_ref, w_ref,
                   k_ref, v_ref, ckvn_s, kr_s, cos_s, sin_s):
    @pl.when(pl.program_id(1) == 0)
    def _():
        c = ckv_ref[...].astype(F32)
        r = lax.rsqrt(jnp.mean(c * c, axis=-1, keepdims=True) + EPS)
        ckvn_s[...] = (c * r * glat_ref[...]).astype(BF16)
        _rope_tables(pos_ref, invf_ref, sgn_ref, cos_s, sin_s)
        kr_s[...] = _rope_pair(kr_ref[...].astype(F32), gkr_ref[...], cos_s[...], sin_s[...]).astype(BF16)

    for g in range(k_ref.shape[0]):
        kv = jnp.dot(ckvn_s[...], w_ref[:, g * HEAD_COLS:(g + 1) * HEAD_COLS], preferred_element_type=F32)
        kn = kv[:, :NOPE_DIM]
        rn = lax.rsqrt(jnp.mean(kn * kn, axis=-1, keepdims=True) + EPS)
        k_ref[g, :, :NOPE_DIM] = (kn * rn * gkn_ref[...]).astype(k_ref.dtype)
        k_ref[g, :, NOPE_DIM:] = kr_s[:, :ROPE_DIM]
        v_ref[g] = kv[:, NOPE_DIM:].astype(v_ref.dtype)


def _rope_consts():
    inv_freq = ROPE_THETA ** (-jnp.arange(0, ROPE_DIM, 2, dtype=F32) / ROPE_DIM)
    invf = jnp.tile(inv_freq, LANES // (ROPE_DIM // 2)).reshape(1, LANES)
    half = jnp.concatenate([-jnp.ones((ROPE_DIM // 2,), F32), jnp.ones((ROPE_DIM // 2,), F32)])
    sgn = jnp.tile(half, LANES // ROPE_DIM).reshape(1, LANES)
    return invf, sgn


def _swap_halves(g):
    return jnp.concatenate([g[..., ROPE_DIM // 2:], g[..., :ROPE_DIM // 2]], axis=-1)


def _mla_projections(small, pos, b, s, heads, q_lora, kv_lora, q_lat_g, kv_lat_g, wq, wkv,
                     qn_nope_g, qn_rope_g, kn_nope_g, kn_rope_g, tm=512, hps=8):
    n = b * s
    tm = _tile(s, tm)
    spb = s // tm
    invf, sgn = _rope_consts()
    scale = float(QK_DIM) ** -0.5 * LOG2_E
    gq = jnp.concatenate([qn_nope_g, qn_rope_g, _swap_halves(qn_rope_g)]).reshape(1, HEAD_COLS)
    gkr = jnp.concatenate([kn_rope_g, _swap_halves(kn_rope_g)]).reshape(1, LANES)
    hps = _tile(heads, hps)
    row = lambda i, h: (i, 0)
    const = lambda i, h: (0, 0)
    head_out = lambda i, h: (i // spb, h, i % spb, 0)
    assert q_lora % kv_lora == 0 and kv_lora % LANES == 0
    ckv_blk = q_lora // kv_lora
    kr_blk = (q_lora + kv_lora) // LANES

    q = pl.pallas_call(
        functools.partial(_qproj_kernel, scale=scale),
        grid=(n // tm, heads // hps),
        in_specs=[pl.BlockSpec((tm, q_lora), row), pl.BlockSpec((tm, 1), row),
                  pl.BlockSpec((1, LANES), const), pl.BlockSpec((1, LANES), const),
                  pl.BlockSpec((1, q_lora), const), pl.BlockSpec((1, HEAD_COLS), const),
                  pl.BlockSpec((q_lora, hps * HEAD_COLS), lambda i, h: (0, h))],
        out_specs=pl.BlockSpec((None, hps, tm, QK_DIM), head_out),
        out_shape=jax.ShapeDtypeStruct((b, heads, s, QK_DIM), BF16),
        scratch_shapes=[pltpu.VMEM((tm, q_lora), BF16), pltpu.VMEM((tm, LANES), F32), pltpu.VMEM((tm, LANES), F32)],
        compiler_params=_params("parallel", "arbitrary"),
        name="mla_q_proj",
    )(small, pos, invf, sgn, q_lat_g.reshape(1, q_lora), gq, wq)

    k, v = pl.pallas_call(
        _kvproj_kernel,
        grid=(n // tm, heads // hps),
        in_specs=[pl.BlockSpec((tm, kv_lora), lambda i, h: (i, ckv_blk)),
                  pl.BlockSpec((tm, LANES), lambda i, h: (i, kr_blk)),
                  pl.BlockSpec((tm, 1), row),
                  pl.BlockSpec((1, LANES), const), pl.BlockSpec((1, LANES), const),
                  pl.BlockSpec((1, kv_lora), const), pl.BlockSpec((1, LANES), const),
                  pl.BlockSpec((1, NOPE_DIM), const),
                  pl.BlockSpec((kv_lora, hps * HEAD_COLS), lambda i, h: (0, h))],
        out_specs=[pl.BlockSpec((None, hps, tm, QK_DIM), head_out),
                   pl.BlockSpec((None, hps, tm, V_DIM), head_out)],
        out_shape=[jax.ShapeDtypeStruct((b, heads, s, QK_DIM), BF16),
                   jax.ShapeDtypeStruct((b, heads, s, V_DIM), BF16)],
        scratch_shapes=[pltpu.VMEM((tm, kv_lora), BF16), pltpu.VMEM((tm, LANES), BF16),
                        pltpu.VMEM((tm, LANES), F32), pltpu.VMEM((tm, LANES), F32)],
        compiler_params=_params("parallel", "arbitrary"),
        name="mla_kv_proj",
    )(small, small, pos, invf, sgn, kv_lat_g.reshape(1, kv_lora), gkr, kn_nope_g.reshape(1, NOPE_DIM), wkv)
    return q, k, v


def _flash_kernel(q_ref, k_ref, v_ref, o_ref, s_a, s_b, m_s, l_s, acc_s, *, tq, tk):
    n_q = q_ref.shape[0] // tq

    def scores(j, dst, q0, r0=0):
        ks = k_ref[pl.ds(pl.multiple_of(j * tk, tk), tk), :]
        dst[r0:, :] = lax.dot_general(q_ref[q0 + r0:q0 + tq, :], ks, (((1,), (1,)), ((), ())),
                                      preferred_element_type=F32)

    def soft_pv(src, j, q0, r0=0, diagonal=False):
        rows = slice(q0 + r0, q0 + tq)
        s = src[r0:, :]
        if diagonal:
            row = lax.broadcasted_iota(I32, s.shape, 0)
            col = lax.broadcasted_iota(I32, s.shape, 1)
            s = jnp.where(col <= row, s, -jnp.inf)
        m_prev = m_s[rows, :]
        m_new = jnp.maximum(m_prev, jnp.max(s, axis=-1, keepdims=True))
        alpha = jnp.exp2(m_prev - m_new)
        ps = [jnp.exp2(s[:, c * LANES:(c + 1) * LANES] - m_new) for c in range(tk // LANES)]
        part = ps[0]
        for x in ps[1:]:
            part = part + x
        l_s[rows, :] = alpha * l_s[rows, :] + part
        m_s[rows, :] = m_new
        vs = v_ref[pl.ds(pl.multiple_of(j * tk, tk), tk), :]
        p = jnp.concatenate(ps, axis=1).astype(BF16)
        acc_s[rows, :] = alpha * acc_s[rows, :] + jnp.dot(p, vs, preferred_element_type=F32)

    m_s[...] = jnp.full_like(m_s, -jnp.inf)
    l_s[...] = jnp.zeros_like(l_s)
    acc_s[...] = jnp.zeros_like(acc_s)

    for i in range(n_q):
        q0 = i * tq
        scores(0, s_a, q0)

        def pair(jj, c, q0=q0):
            j = 2 * jj
            scores(j + 1, s_b, q0)
            soft_pv(s_a, j, q0)
            scores(j + 2, s_a, q0)
            soft_pv(s_b, j + 1, q0)
            return c

        if i:
            lax.fori_loop(0, i, pair, 0)
        scores(2 * i + 1, s_b, q0, tk)
        soft_pv(s_a, 2 * i, q0, 0, True)
        soft_pv(s_b, 2 * i + 1, q0, tk, True)
        l = jnp.sum(l_s[q0:q0 + tq, :], axis=-1, keepdims=True)
        o_ref[q0:q0 + tq, :] = (acc_s[q0:q0 + tq, :] / l).astype(o_ref.dtype)


def _flash_attention(q, k, v, tq=1024):
    b, heads, s, _ = q.shape
    tq = _tile(s, tq)
    tk = tq // 2
    assert tk % LANES == 0
    whole = lambda bi, h: (bi, h, 0, 0)
    return pl.pallas_call(
        functools.partial(_flash_kernel, tq=tq, tk=tk),
        grid=(b, heads),
        in_specs=[pl.BlockSpec((None, None, s, QK_DIM), whole), pl.BlockSpec((None, None, s, QK_DIM), whole),
                  pl.BlockSpec((None, None, s, V_DIM), whole)],
        out_specs=pl.BlockSpec((s, V_DIM), lambda bi, h: (bi, h)),
        out_shape=jax.ShapeDtypeStruct((b * s, heads * V_DIM), BF16),
        scratch_shapes=[pltpu.VMEM((tq, tk), F32), pltpu.VMEM((tq, tk), F32),
                        pltpu.VMEM((s, LANES), F32), pltpu.VMEM((s, LANES), F32), pltpu.VMEM((s, V_DIM), F32)],
        compiler_params=_params("parallel", "parallel"),
        name="mla_flash_attention",
    )(q, k, v)


def _conv_kernel(cb_ref, cc_ref, ch_ref, pc_ref, ph_ref, w_ref, o_ref, *, tiles_per_seq):
    first = (pl.program_id(0) % tiles_per_seq) == 0
    u = cc_ref[...].astype(F32) * ch_ref[...].astype(F32)
    halo = pc_ref[...].astype(F32) * ph_ref[...].astype(F32)
    halo = jnp.where(first, 0.0, halo)
    cat = jnp.concatenate([halo, u], axis=0)
    n_halo = halo.shape[0]
    u1 = pltpu.roll(cat, 1, 0)[n_halo:]
    u2 = pltpu.roll(cat, 2, 0)[n_halo:]
    y = w_ref[0:1, :] * u2 + w_ref[1:2, :] * u1 + w_ref[2:3, :] * u
    o_ref[...] = (cb_ref[...].astype(F32) * y).astype(o_ref.dtype)


def _short_conv(big, conv_w, n, s, d, ts=512, tc=1024):
    ts, tc = _tile(s, ts), _tile(d, tc)
    halo = 8
    cpb = d // tc
    seg = lambda k: (lambda i, j: (i, k * cpb + j))
    prev = lambda k: (lambda i, j: (jnp.maximum(i * (ts // halo) - 1, 0), k * cpb + j))
    return pl.pallas_call(
        functools.partial(_conv_kernel, tiles_per_seq=s // ts),
        grid=(n // ts, cpb),
        in_specs=[pl.BlockSpec((ts, tc), seg(0)), pl.BlockSpec((ts, tc), seg(1)), pl.BlockSpec((ts, tc), seg(2)),
                  pl.BlockSpec((halo, tc), prev(1)), pl.BlockSpec((halo, tc), prev(2)),
                  pl.BlockSpec((conv_w.shape[0], tc), lambda i, j: (0, j))],
        out_specs=pl.BlockSpec((ts, tc), lambda i, j: (i, j)),
        out_shape=jax.ShapeDtypeStruct((n, d), BF16),
        compiler_params=_params("parallel", "parallel"),
        name="short_conv",
    )(big, big, big, big, big, conv_w)


def _memattn_kernel(q_ref, k_ref, v_ref, gq_ref, gk_ref, o_ref, kn_s, *, scale):
    @pl.when(pl.program_id(2) == 0)
    def _():
        k = k_ref[...].astype(F32)
        r = lax.rsqrt(jnp.mean(k * k, axis=-1, keepdims=True) + EPS)
        kn_s[...] = (k * r * gk_ref[...]).astype(BF16)

    q = q_ref[...].astype(F32)
    r = lax.rsqrt(jnp.mean(q * q, axis=-1, keepdims=True) + EPS)
    qn = (q * (r * scale) * gq_ref[...]).astype(BF16)
    s = lax.dot_general(qn, kn_s[...], (((1,), (1,)), ((), ())), preferred_element_type=F32)
    m = jnp.max(s, axis=-1, keepdims=True)
    p = jnp.exp(s - m)
    l = jnp.sum(p, axis=-1, keepdims=True)
    o = jnp.dot(p.astype(BF16), v_ref[...], preferred_element_type=F32)
    o_ref[...] = (o / l).astype(o_ref.dtype)


def _memory_attention(big, kvm, gq, gk, b, s, d, mem_len, q_seg, ts=512):
    hd = d // MEM_HEADS
    ts = _tile(s, ts)
    spb = s // ts
    return pl.pallas_call(
        functools.partial(_memattn_kernel, scale=float(hd) ** -0.5),
        grid=(b, MEM_HEADS, spb),
        in_specs=[pl.BlockSpec((ts, hd), lambda bi, h, i: (bi * spb + i, q_seg * MEM_HEADS + h)),
                  pl.BlockSpec((mem_len, hd), lambda bi, h, i: (bi, h)),
                  pl.BlockSpec((mem_len, hd), lambda bi, h, i: (bi, MEM_HEADS + h)),
                  pl.BlockSpec((1, hd), lambda bi, h, i: (0, 0)),
                  pl.BlockSpec((1, hd), lambda bi, h, i: (0, 0))],
        out_specs=pl.BlockSpec((ts, hd), lambda bi, h, i: (bi * spb + i, h)),
        out_shape=jax.ShapeDtypeStruct((b * s, d), BF16),
        scratch_shapes=[pltpu.VMEM((mem_len, hd), BF16)],
        compiler_params=_params("parallel", "parallel", "arbitrary"),
        name="memory_attention",
    )(big, kvm, kvm, gq.reshape(1, hd), gk.reshape(1, hd))


def _merge_kernel(a0_ref, a1_ref, a2_ref, w_ref, g_ref, o_ref, acc_s):
    br = pl.program_id(2)
    gate = jax.nn.sigmoid(g_ref[...].astype(F32))

    @pl.when(br == 0)
    def _():
        acc_s[...] = gate * jnp.dot(a0_ref[...], w_ref[...], preferred_element_type=F32)

    @pl.when(br == 1)
    def _():
        acc_s[...] += gate * jnp.dot(a1_ref[...], w_ref[...], preferred_element_type=F32)

    @pl.when(br == 2)
    def _():
        o_ref[...] = (acc_s[...] + gate * jnp.dot(a2_ref[...], w_ref[...], preferred_element_type=F32)).astype(o_ref.dtype)


def _gated_merge(a_mla, a_conv, a_mem, w_branch, big, gate_seg, tm=512, tn=1024):
    n, d = a_mla.shape
    tm, tn = _tile(n, tm), _tile(d, tn)
    cpb = d // tn
    lhs = pl.BlockSpec((tm, d), lambda i, j, br: (i, 0))
    return pl.pallas_call(
        _merge_kernel,
        grid=(n // tm, cpb, 3),
        in_specs=[lhs, lhs, lhs,
                  pl.BlockSpec((None, d, tn), lambda i, j, br: (br, 0, j)),
                  pl.BlockSpec((tm, tn), lambda i, j, br: (i, (gate_seg + br) * cpb + j))],
        out_specs=pl.BlockSpec((tm, tn), lambda i, j, br: (i, j)),
        out_shape=jax.ShapeDtypeStruct((n, d), BF16),
        scratch_shapes=[pltpu.VMEM((tm, tn), F32)],
        compiler_params=_params("parallel", "parallel", "arbitrary"),
        name="gated_merge",
    )(a_mla, a_conv, a_mem, w_branch, big)


def _router_kernel(x_ref, g_ref, wr_ref, br_ref, hp_ref, idx_ref, wt_ref, rank_ref, cnt_ref, carry_s, *, n_exp):
    tm = x_ref.shape[0]

    @pl.when(pl.program_id(0) == 0)
    def _():
        carry_s[...] = jnp.zeros_like(carry_s)

    x = x_ref[...]
    r = lax.rsqrt(jnp.mean(x * x, axis=-1, keepdims=True) + EPS)
    h = x * r * g_ref[...]
    hp_ref[...] = _pack_halves(h)

    logits = jnp.dot(h, wr_ref[...], preferred_element_type=F32, precision=lax.Precision.HIGHEST) + br_ref[...]
    lane = lax.broadcasted_iota(I32, (tm, n_exp), 1)
    out_lane = lax.broadcasted_iota(I32, (tm, LANES), 1)
    work = logits
    vals, hots = [], []
    idx_out = jnp.zeros((tm, LANES), I32)
    for k in range(TOP_K):
        m = jnp.max(work, axis=-1, keepdims=True)
        idx = jnp.min(jnp.where(work == m, lane, n_exp), axis=-1, keepdims=True)
        hot = lane == idx
        vals.append(m)
        hots.append(hot)
        idx_out = jnp.where(out_lane == k, idx, idx_out)
        work = jnp.where(hot, -jnp.inf, work)
    idx_ref[...] = idx_out

    exps = [jnp.exp(v - vals[0]) for v in vals]
    denom = exps[0]
    for e in exps[1:]:
        denom = denom + e
    wt_out = jnp.zeros((tm, LANES), F32)
    for k in range(TOP_K):
        wt_out = jnp.where(out_lane == k, exps[k] / denom, wt_out)
    wt_ref[...] = wt_out

    member = jnp.zeros((tm, n_exp), F32)
    for hot in hots:
        member = member + hot.astype(F32)
    tri = (lax.broadcasted_iota(I32, (tm, tm), 1) < lax.broadcasted_iota(I32, (tm, tm), 0)).astype(BF16)
    prefix = jnp.dot(tri, member.astype(BF16), preferred_element_type=F32) + carry_s[0:1, 0:n_exp]
    rank_out = jnp.zeros((tm, LANES), I32)
    for k in range(TOP_K):
        rk = jnp.sum(jnp.where(hots[k], prefix, 0.0), axis=-1, keepdims=True).astype(I32)
        rank_out = jnp.where(out_lane == k, rk, rank_out)
    rank_ref[...] = rank_out

    carry_s[0:1, 0:n_exp] = carry_s[0:1, 0:n_exp] + jnp.sum(member, axis=0, keepdims=True)
    cnt_ref[...] = carry_s[...]


def _router(x1, g, w_router, b_router, tm=512):
    n, d = x1.shape
    n_exp = w_router.shape[1]
    assert n_exp <= LANES
    tm = _tile(n, tm)
    row = lambda i: (i, 0)
    const = lambda i: (0, 0)
    return pl.pallas_call(
        functools.partial(_router_kernel, n_exp=n_exp),
        grid=(n // tm,),
        in_specs=[pl.BlockSpec((tm, d), row), pl.BlockSpec((1, d), const),
                  pl.BlockSpec((d, n_exp), const), pl.BlockSpec((1, n_exp), const)],
        out_specs=[pl.BlockSpec((tm, d // 2), row), pl.BlockSpec((tm, LANES), row), pl.BlockSpec((tm, LANES), row),
                   pl.BlockSpec((tm, LANES), row), pl.BlockSpec((8, LANES), const)],
        out_shape=[jax.ShapeDtypeStruct((n, d // 2), U32), jax.ShapeDtypeStruct((n, LANES), I32),
                   jax.ShapeDtypeStruct((n, LANES), F32), jax.ShapeDtypeStruct((n, LANES), I32),
                   jax.ShapeDtypeStruct((8, LANES), F32)],
        scratch_shapes=[pltpu.VMEM((8, LANES), F32)],
        compiler_params=_params("arbitrary"),
        name="router",
    )(x1, g.reshape(1, d), w_router, b_router.reshape(1, n_exp))


def _dispatch_kernel(dest_ref, zflag_ref, hp_ref, xs_ref, zeros_s, sem, zsem, *, rows):
    base = pl.program_id(0) * rows
    te = zeros_s.shape[0]

    @pl.when(pl.program_id(0) == 0)
    def _():
        zeros_s[...] = jnp.zeros_like(zeros_s)

        def fill(t):
            return pltpu.make_async_copy(zeros_s, xs_ref.at[pl.ds(pl.multiple_of(t * te, te), te)], zsem)

        def start(t, c):
            @pl.when(zflag_ref[t] != 0)
            def _():
                fill(t).start()
            return c

        def finish(t, c):
            @pl.when(zflag_ref[t] != 0)
            def _():
                fill(t).wait()
            return c

        lax.fori_loop(0, zflag_ref.shape[0], start, 0)
        lax.fori_loop(0, zflag_ref.shape[0], finish, 0)

    def issue(r, c):
        for k in range(TOP_K):
            d = dest_ref[(base + r) * TOP_K + k]
            pltpu.make_async_copy(hp_ref.at[pl.ds(r, 1)], xs_ref.at[pl.ds(d, 1)], sem).start()
        return c

    lax.fori_loop(0, rows, issue, 0)

    def drain(r, c):
        for k in range(TOP_K):
            pltpu.make_async_copy(hp_ref.at[pl.ds(0, 1)], xs_ref.at[pl.ds(0, 1)], sem).wait()
        return c

    lax.fori_loop(0, rows, drain, 0)


def _dispatch(dest_flat, zero_flag, hp, te, rows=256):
    n, half = hp.shape
    rows = _tile(n, rows)
    n_slots = zero_flag.shape[0] * te
    return pl.pallas_call(
        functools.partial(_dispatch_kernel, rows=rows),
        grid_spec=pltpu.PrefetchScalarGridSpec(
            num_scalar_prefetch=2,
            grid=(n // rows,),
            in_specs=[pl.BlockSpec((rows, half), lambda i, dest, zflag: (i, 0))],
            out_specs=pl.BlockSpec(memory_space=pl.ANY),
            scratch_shapes=[pltpu.VMEM((te, half), U32), pltpu.SemaphoreType.DMA(()), pltpu.SemaphoreType.DMA(())],
        ),
        out_shape=jax.ShapeDtypeStruct((n_slots, half), U32),
        compiler_params=_params("arbitrary"),
        name="expert_dispatch",
    )(dest_flat, zero_flag, hp)


def _group_items(tile_e, first_tile, tiles_of, rows_of, n_used, n_blocks, n_tiles, te):
    n_items = n_used * n_blocks
    w = jnp.arange(n_tiles * n_blocks, dtype=I32)
    wc = jnp.minimum(w, n_items - 1)
    e = tile_e[wc // n_blocks]
    first, count = first_tile[e], jnp.maximum(tiles_of[e], 1)
    r = wc - first * n_blocks
    in_tile, in_blk = first + r % count, r // count
    z = jnp.maximum(w - n_items, 0)
    live = w < n_items
    out_tile = jnp.where(live, in_tile, n_used + z // n_blocks)
    out_blk = jnp.where(live, in_blk, z % n_blocks)
    starts = (live & (r % count == 0)).astype(I32)
    real = rows_of[e] - (in_tile - first) * te
    fill = jnp.where(real <= 0, 2, jnp.where(real <= te // 2, 1, 0)).astype(I32)
    return in_tile, in_blk, e, starts, fill, out_tile, out_blk, n_items.reshape(1)


def _expert_up_kernel(tile_ref, blk_ref, exp_ref, start_ref, fill_ref, ot_ref, ob_ref, ni_ref,
                      xs_ref, wg_ref, bg_ref, wu_ref, bu_ref, act_ref, xb_s, wg_s, wu_s):
    w = pl.program_id(0)
    live = w < ni_ref[0]
    te, half = xs_ref.shape

    @pl.when(start_ref[w] != 0)
    def _():
        wg_s[...] = wg_ref[...].astype(BF16)
        wu_s[...] = wu_ref[...].astype(BF16)

    def compute(rows):
        lo, hi = _unpack_halves(xs_ref[:rows, :])
        xb_s[:rows, :half] = lo.astype(BF16)
        xb_s[:rows, half:] = hi.astype(BF16)
        xb = xb_s[:rows, :]
        g = jnp.dot(xb, wg_s[...], preferred_element_type=F32) + bg_ref[...]
        u = jnp.dot(xb, wu_s[...], preferred_element_type=F32) + bu_ref[...]
        g = jnp.minimum(g, SWIGLU_LIMIT)
        u = jnp.clip(u, -SWIGLU_LIMIT, SWIGLU_LIMIT)
        act_ref[:rows, :] = (g * jax.nn.sigmoid(SWIGLU_ALPHA * g) * (u + 1.0)).astype(act_ref.dtype)
        if rows < te:
            act_ref[rows:, :] = jnp.zeros((te - rows, act_ref.shape[1]), act_ref.dtype)

    @pl.when(jnp.logical_and(live, fill_ref[w] == 0))
    def _():
        compute(te)

    @pl.when(jnp.logical_and(live, fill_ref[w] == 1))
    def _():
        compute(te // 2)

    @pl.when(jnp.logical_or(jnp.logical_not(live), fill_ref[w] == 2))
    def _():
        act_ref[...] = jnp.zeros_like(act_ref)


def _expert_down_kernel(tile_ref, blk_ref, exp_ref, start_ref, fill_ref, ot_ref, ob_ref, ni_ref,
                        act_ref, wdl_ref, wdh_ref, bdl_ref, bdh_ref, o_ref, wl_s, wh_s):
    w = pl.program_id(0)
    live = w < ni_ref[0]
    te = act_ref.shape[0]

    @pl.when(start_ref[w] != 0)
    def _():
        wl_s[...] = wdl_ref[...].astype(BF16)
        wh_s[...] = wdh_ref[...].astype(BF16)

    def compute(rows):
        a = act_ref[:rows, :]
        lo = jnp.dot(a, wl_s[...], preferred_element_type=F32) + bdl_ref[...]
        hi = jnp.dot(a, wh_s[...], preferred_element_type=F32) + bdh_ref[...]
        o_ref[:rows, :] = _pack_pair(lo, hi)
        if rows < te:
            o_ref[rows:, :] = jnp.zeros((te - rows, o_ref.shape[1]), o_ref.dtype)

    @pl.when(jnp.logical_and(live, fill_ref[w] == 0))
    def _():
        compute(te)

    @pl.when(jnp.logical_and(live, fill_ref[w] == 1))
    def _():
        compute(te // 2)

    @pl.when(jnp.logical_or(jnp.logical_not(live), fill_ref[w] == 2))
    def _():
        o_ref[...] = jnp.zeros_like(o_ref)


def _expert_mlp(tile_e, first_tile, tiles_of, rows_of, n_used, xs, wg, bg, wu, bu, wd, bd, te,
                up_rows=256, tcu=512, tnd=1024):
    n_slots, half = xs.shape
    d = 2 * half
    n_exp, _, de = wg.shape
    tcu, tnd = _tile(de, tcu), _tile(half, tnd)
    n_up, n_down = de // tcu, half // tnd
    n_tiles = n_slots // te
    up_rows = _tile(te, up_rows)
    sub = te // up_rows
    n_pre = 8

    in_row = lambda w, tl, bk, ex, st, sh, ot, ob, ni: (tl[w], 0)
    w_blk = lambda w, tl, bk, ex, st, sh, ot, ob, ni: (ex[w], 0, bk[w])
    out_blk = lambda w, tl, bk, ex, st, sh, ot, ob, ni: (ot[w], ob[w])

    up_items = _group_items(jnp.repeat(tile_e, sub), first_tile * sub, tiles_of * sub, rows_of, n_used * sub,
                            n_up, n_tiles * sub, up_rows)
    act = pl.pallas_call(
        _expert_up_kernel,
        grid_spec=pltpu.PrefetchScalarGridSpec(
            num_scalar_prefetch=n_pre,
            grid=(n_tiles * sub * n_up,),
            in_specs=[pl.BlockSpec((up_rows, half), in_row),
                      pl.BlockSpec((None, d, tcu), w_blk), pl.BlockSpec((None, 1, tcu), w_blk),
                      pl.BlockSpec((None, d, tcu), w_blk), pl.BlockSpec((None, 1, tcu), w_blk)],
            out_specs=pl.BlockSpec((up_rows, tcu), out_blk),
            scratch_shapes=[pltpu.VMEM((up_rows, d), BF16), pltpu.VMEM((d, tcu), BF16), pltpu.VMEM((d, tcu), BF16)],
        ),
        out_shape=jax.ShapeDtypeStruct((n_slots, de), BF16),
        compiler_params=_params("arbitrary"),
        name="expert_gate_up",
    )(*up_items, xs, wg, bg.reshape(n_exp, 1, de), wu, bu.reshape(n_exp, 1, de))

    hi_blk = lambda w, tl, bk, ex, st, sh, ot, ob, ni: (ex[w], 0, n_down + bk[w])
    bd3 = bd.reshape(n_exp, 1, d)
    down_items = _group_items(tile_e, first_tile, tiles_of, rows_of, n_used, n_down, n_tiles, te)
    return pl.pallas_call(
        _expert_down_kernel,
        grid_spec=pltpu.PrefetchScalarGridSpec(
            num_scalar_prefetch=n_pre,
            grid=(n_tiles * n_down,),
            in_specs=[pl.BlockSpec((te, de), in_row),
                      pl.BlockSpec((None, de, tnd), w_blk), pl.BlockSpec((None, de, tnd), hi_blk),
                      pl.BlockSpec((None, 1, tnd), w_blk), pl.BlockSpec((None, 1, tnd), hi_blk)],
            out_specs=pl.BlockSpec((te, tnd), out_blk),
            scratch_shapes=[pltpu.VMEM((de, tnd), BF16), pltpu.VMEM((de, tnd), BF16)],
        ),
        out_shape=jax.ShapeDtypeStruct((n_slots, half), U32),
        compiler_params=_params("arbitrary"),
        name="expert_down",
    )(*down_items, act, wd, wd, bd3, bd3)


def _combine_kernel(dest_ref, ys_ref, x_ref, wt_ref, o_ref, gbuf, sem, *, rows):
    base = pl.program_id(0) * rows
    half = gbuf.shape[2]

    def issue(r, c):
        for k in range(TOP_K):
            d = dest_ref[(base + r) * TOP_K + k]
            pltpu.make_async_copy(ys_ref.at[pl.ds(d, 1)], gbuf.at[k, pl.ds(r, 1)], sem).start()
        return c

    lax.fori_loop(0, rows, issue, 0)

    def drain(r, c):
        for k in range(TOP_K):
            pltpu.make_async_copy(ys_ref.at[pl.ds(0, 1)], gbuf.at[k, pl.ds(0, 1)], sem).wait()
        return c

    lax.fori_loop(0, rows, drain, 0)

    lo_sum = x_ref[:, :half]
    hi_sum = x_ref[:, half:]
    for k in range(TOP_K):
        lo, hi = _unpack_halves(gbuf[k])
        wk = wt_ref[:, k:k + 1]
        lo_sum = lo_sum + wk * lo
        hi_sum = hi_sum + wk * hi
    o_ref[:, :half] = lo_sum
    o_ref[:, half:] = hi_sum


def _combine(dest_flat, ys, x1, wt, rows=256):
    n, d = x1.shape
    rows = _tile(n, rows)
    return pl.pallas_call(
        functools.partial(_combine_kernel, rows=rows),
        grid_spec=pltpu.PrefetchScalarGridSpec(
            num_scalar_prefetch=1,
            grid=(n // rows,),
            in_specs=[pl.BlockSpec(memory_space=pl.ANY),
                      pl.BlockSpec((rows, d), lambda i, dest: (i, 0)),
                      pl.BlockSpec((rows, LANES), lambda i, dest: (i, 0))],
            out_specs=pl.BlockSpec((rows, d), lambda i, dest: (i, 0)),
            scratch_shapes=[pltpu.VMEM((TOP_K, rows, d // 2), U32), pltpu.SemaphoreType.DMA(())],
        ),
        out_shape=jax.ShapeDtypeStruct((n, d), F32),
        compiler_params=_params("arbitrary"),
        name="expert_combine",
    )(dest_flat, ys, x1, wt)


def _routed_experts(x1, norm_g, w_router, b_router, wg, bg, wu, bu, wd, bd, te=512):
    n, d = x1.shape
    n_exp = w_router.shape[1]
    hp, idx_p, wt_p, rank_p, cnt = _router(x1, norm_g, w_router, b_router)
    top_idx = idx_p[:, :TOP_K]
    rank = rank_p[:, :TOP_K]

    te = min(te, n * TOP_K)
    counts = cnt[0, :n_exp].astype(I32)
    padded = ((counts + te - 1) // te) * te
    pend = jnp.cumsum(padded)
    pstart = pend - padded
    n_tiles = (n * TOP_K) // te + n_exp
    n_used = (pend[-1] // te).astype(I32)
    tiles = jnp.arange(n_tiles, dtype=I32)
    tile_e = jnp.minimum(jnp.sum((pend[None, :] <= (tiles * te)[:, None]).astype(I32), axis=1), n_exp - 1)
    dest = (pstart[top_idx] + rank).reshape(n * TOP_K).astype(I32)
    partial_last = jnp.any((tiles[:, None] == (pend // te - 1)[None, :]) & ((counts % te) != 0)[None, :], axis=1)
    zero_flag = (partial_last | (tiles >= n_used)).astype(I32)

    xs = _dispatch(dest, zero_flag, hp, te)
    ys = _expert_mlp(tile_e, pstart // te, padded // te, counts, n_used, xs, wg, bg, wu, bu, wd, bd, te)
    return _combine(dest, ys, x1, wt_p)


def kernel(x, mem, positions, norm1_g, w_in, q_lat_g, kv_lat_g, w_uq, w_ukv, qn_nope_g, qn_rope_g, kn_nope_g, kn_rope_g, conv_w, mem_norm_g, w_mem_kv, mem_qn_g, mem_kn_g, w_branch, w_out, norm2_g, w_router, b_router, w_gate, b_gate, w_up, b_up, w_down, b_down):
    b, s, d = x.shape
    n = b * s
    depth = w_in.shape[0]
    q_lora = w_uq.shape[1]
    kv_lora = w_ukv.shape[1]
    heads = w_ukv.shape[2] // (NOPE_DIM + V_DIM)
    mem_len = mem.shape[1]
    assert heads * V_DIM == d and d % (2 * LANES) == 0
    lat = q_lora + kv_lora
    kr_end = lat + ROPE_DIM
    assert w_in.shape[2] == kr_end + 7 * d

    pos = positions.reshape(n, 1).astype(I32)
    xf = x.reshape(n, d)
    mem_f = mem.reshape(b * mem_len, d)

    for l in range(depth):
        w_l = w_in[l]
        w_t = jnp.swapaxes(w_l, 0, 1).astype(BF16)
        w_small_t = jnp.concatenate(
            [w_t[:kr_end], w_t[lat + ROPE_DIM // 2:kr_end], w_t[lat:lat + ROPE_DIM // 2]], axis=0)
        wq3 = w_uq[l].reshape(q_lora, heads, QK_DIM)
        wq = jnp.concatenate([wq3, wq3[..., NOPE_DIM + ROPE_DIM // 2:], wq3[..., NOPE_DIM:NOPE_DIM + ROPE_DIM // 2]],
                             axis=-1).reshape(q_lora, heads * HEAD_COLS).astype(BF16)
        wkv = w_ukv[l].astype(BF16)

        xn = _rmsnorm(xf, norm1_g[l])
        big = _matmul_nt(xn, w_t, BF16, row0=kr_end, n=7 * d, name="in_proj_wide")
        small = _matmul_nt(xn, w_small_t, BF16, tn=w_small_t.shape[0], name="in_proj_latent")

        q, k, v = _mla_projections(small, pos, b, s, heads, q_lora, kv_lora, q_lat_g[l], kv_lat_g[l], wq, wkv,
                                   qn_nope_g[l], qn_rope_g[l], kn_nope_g[l], kn_rope_g[l])
        a_mla = _flash_attention(q, k, v)

        a_conv = _short_conv(big, conv_w[l], n, s, d)

        mem_n = _rmsnorm(mem_f, mem_norm_g[l])
        kvm = _matmul(mem_n, w_mem_kv[l].astype(BF16), BF16, name="mem_kv_proj")
        a_mem = _memory_attention(big, kvm, mem_qn_g[l], mem_kn_g[l], b, s, d, mem_len, q_seg=3)

        merged = _gated_merge(a_mla, a_conv, a_mem, w_branch[l].astype(BF16), big, gate_seg=4)
        x1 = _matmul(merged, w_out[l].astype(BF16), F32, res=xf, name="out_proj")

        xf = _routed_experts(x1, norm2_g[l], w_router[l], b_router[l],
                             w_gate[l], b_gate[l], w_up[l], b_up[l], w_down[l], b_down[l])
    return xf.reshape(b, s, d)
```

```python
import functools
import math

import jax
import jax.numpy as jnp
from jax import lax
from jax.experimental import pallas as pl
from jax.experimental.pallas import tpu as pltpu

F32 = jnp.float32
BF16 = jnp.bfloat16
U32 = jnp.uint32
I32 = jnp.int32

EPS = 1e-6
NOPE_DIM = 128
ROPE_DIM = 64
V_DIM = 128
QK_DIM = NOPE_DIM + ROPE_DIM
HEAD_COLS = 256
MEM_HEADS = 4
TOP_K = 4
ROPE_THETA = 10000.0
SWIGLU_LIMIT = 7.0
SWIGLU_ALPHA = 1.702
LANES = 128
V7X_VMEM_LIMIT_BYTES = 56 * 1024 * 1024
HI_MASK = 0xFFFF0000
LOG2_E = 1.4426950408889634


def _params(*sem):
    return pltpu.CompilerParams(dimension_semantics=sem, vmem_limit_bytes=V7X_VMEM_LIMIT_BYTES)


def _tile(n, want):
    t = min(n, want)
    while n % t:
        t //= 2
    assert t >= 1
    return t


def _bits(x):
    return lax.bitcast_convert_type(x, U32)


def _pack_pair(lo, hi):
    lo_bits = _bits(lo.astype(BF16).astype(F32)) >> 16
    hi_bits = _bits(hi.astype(BF16).astype(F32)) & jnp.uint32(HI_MASK)
    return lo_bits | hi_bits


def _pack_halves(y):
    half = y.shape[1] // 2
    return _pack_pair(y[:, :half], y[:, half:])


def _unpack_halves(p):
    lo = lax.bitcast_convert_type(p << 16, F32)
    hi = lax.bitcast_convert_type(p & jnp.uint32(HI_MASK), F32)
    return lo, hi


def _rmsnorm_kernel(x_ref, g_ref, o_ref):
    x = x_ref[...].astype(F32)
    r = lax.rsqrt(jnp.mean(x * x, axis=-1, keepdims=True) + EPS)
    o_ref[...] = (x * r * g_ref[...]).astype(o_ref.dtype)


def _rmsnorm(x, g, tm=512):
    m, d = x.shape
    tm = _tile(m, tm)
    return pl.pallas_call(
        _rmsnorm_kernel,
        grid=(m // tm,),
        in_specs=[pl.BlockSpec((tm, d), lambda i: (i, 0)), pl.BlockSpec((1, d), lambda i: (0, 0))],
        out_specs=pl.BlockSpec((tm, d), lambda i: (i, 0)),
        out_shape=jax.ShapeDtypeStruct((m, d), BF16),
        compiler_params=_params("parallel"),
        name="rmsnorm",
    )(x, g.reshape(1, d))


def _mm_kernel(x_ref, w_ref, o_ref):
    o_ref[...] = jnp.dot(x_ref[...], w_ref[...], preferred_element_type=F32).astype(o_ref.dtype)


def _mm_nt_kernel(x_ref, wt_ref, o_ref):
    o_ref[...] = lax.dot_general(x_ref[...], wt_ref[...], (((1,), (1,)), ((), ())),
                                 preferred_element_type=F32).astype(o_ref.dtype)


def _matmul_nt(x, wt, out_dtype, row0=0, n=None, tm=1024, tn=1024, name="matmul_nt"):
    m, k = x.shape
    n = wt.shape[0] - row0 if n is None else n
    tm, tn = _tile(m, tm), _tile(n, tn)
    return pl.pallas_call(
        _mm_nt_kernel,
        grid=(m // tm, n // tn),
        in_specs=[pl.BlockSpec((tm, k), lambda i, j: (i, 0)),
                  pl.BlockSpec((pl.Element(tn), pl.Element(k)),
                               lambda i, j: (pl.multiple_of(row0 + j * tn, math.gcd(row0, tn)), 0))],
        out_specs=pl.BlockSpec((tm, tn), lambda i, j: (i, j)),
        out_shape=jax.ShapeDtypeStruct((m, n), out_dtype),
        compiler_params=_params("parallel", "parallel"),
        name=name,
    )(x, wt)


def _mm_res_kernel(x_ref, w_ref, r_ref, o_ref):
    o_ref[...] = r_ref[...] + jnp.dot(x_ref[...], w_ref[...], preferred_element_type=F32)


def _matmul(x, w, out_dtype, tm=1024, tn=1024, res=None, name="matmul"):
    m, k = x.shape
    n = w.shape[1]
    tm, tn = _tile(m, tm), _tile(n, tn)
    in_specs = [pl.BlockSpec((tm, k), lambda i, j: (i, 0)), pl.BlockSpec((k, tn), lambda i, j: (0, j))]
    args = [x, w]
    body = _mm_kernel
    if res is not None:
        in_specs.append(pl.BlockSpec((tm, tn), lambda i, j: (i, j)))
        args.append(res)
        body = _mm_res_kernel
    return pl.pallas_call(
        body,
        grid=(m // tm, n // tn),
        in_specs=in_specs,
        out_specs=pl.BlockSpec((tm, tn), lambda i, j: (i, j)),
        out_shape=jax.ShapeDtypeStruct((m, n), out_dtype),
        compiler_params=_params("parallel", "parallel"),
        name=name,
    )(*args)


def _rope_tables(pos_ref, invf_ref, sgn_ref, cos_s, sin_s):
    ang = pos_ref[...].astype(F32) * invf_ref[...]
    cos_s[...] = jnp.cos(ang)
    sin_s[...] = jnp.sin(ang) * sgn_ref[...]


def _rope_pair(slab, gain, cos, sin):
    r = lax.rsqrt(jnp.mean(slab * slab, axis=-1, keepdims=True) + EPS)
    slab = slab * r * gain
    return slab * cos + pltpu.roll(slab, ROPE_DIM, 1) * sin


def _qproj_kernel(cq_ref, pos_ref, invf_ref, sgn_ref, glat_ref, gq_ref, w_ref, o_ref, cqn_s, cos_s, sin_s, *, scale):
    @pl.when(pl.program_id(1) == 0)
    def _():
        c = cq_ref[...].astype(F32)
        r = lax.rsqrt(jnp.mean(c * c, axis=-1, keepdims=True) + EPS)
        cqn_s[...] = (c * r * glat_ref[...]).astype(BF16)
        _rope_tables(pos_ref, invf_ref, sgn_ref, cos_s, sin_s)

    for g in range(o_ref.shape[0]):
        q = jnp.dot(cqn_s[...], w_ref[:, g * HEAD_COLS:(g + 1) * HEAD_COLS], preferred_element_type=F32)
        nope = q[:, :NOPE_DIM]
        rn = lax.rsqrt(jnp.mean(nope * nope, axis=-1, keepdims=True) + EPS)
        nope = nope * rn * gq_ref[:, :NOPE_DIM]
        rot = _rope_pair(q[:, NOPE_DIM:], gq_ref[:, NOPE_DIM:], cos_s[...], sin_s[...])
        o_ref[g, :, :NOPE_DIM] = (nope * scale).astype(o_ref.dtype)
        o_ref[g, :, NOPE_DIM:] = (rot[:, :ROPE_DIM] * scale).astype(o_ref.dtype)


def _kvproj_kernel(ckv_ref, kr_ref, pos_ref, invf_ref, sgn_ref, glat_ref, gkr_ref, gkn_ref, w_ref,
                   k_ref, v_ref, ckvn_s, kr_s, cos_s, sin_s):
    @pl.when(pl.program_id(1) == 0)
    def _():
        c = ckv_ref[...].astype(F32)
        r = lax.rsqrt(jnp.mean(c * c, axis=-1, keepdims=True) + EPS)
        ckvn_s[...] = (c * r * glat_ref[...]).astype(BF16)
        _rope_tables(pos_ref, invf_ref, sgn_ref, cos_s, sin_s)
        kr_s[...] = _rope_pair(kr_ref[...].astype(F32), gkr_ref[...], cos_s[...], sin_s[...]).astype(BF16)

    for g in range(k_ref.shape[0]):
        kv = jnp.dot(ckvn_s[...], w_ref[:, g * HEAD_COLS:(g + 1) * HEAD_COLS], preferred_element_type=F32)
        kn = kv[:, :NOPE_DIM]
        rn = lax.rsqrt(jnp.mean(kn * kn, axis=-1, keepdims=True) + EPS)
        k_ref[g, :, :NOPE_DIM] = (kn * rn * gkn_ref[...]).astype(k_ref.dtype)
        k_ref[g, :, NOPE_DIM:] = kr_s[:, :ROPE_DIM]
        v_ref[g] = kv[:, NOPE_DIM:].astype(v_ref.dtype)


def _rope_consts():
    inv_freq = ROPE_THETA ** (-jnp.arange(0, ROPE_DIM, 2, dtype=F32) / ROPE_DIM)
    invf = jnp.tile(inv_freq, LANES // (ROPE_DIM // 2)).reshape(1, LANES)
    half = jnp.concatenate([-jnp.ones((ROPE_DIM // 2,), F32), jnp.ones((ROPE_DIM // 2,), F32)])
    sgn = jnp.tile(half, LANES // ROPE_DIM).reshape(1, LANES)
    return invf, sgn


def _swap_halves(g):
    return jnp.concatenate([g[..., ROPE_DIM // 2:], g[..., :ROPE_DIM // 2]], axis=-1)


def _mla_projections(small, pos, b, s, heads, q_lora, kv_lora, q_lat_g, kv_lat_g, wq, wkv,
                     qn_nope_g, qn_rope_g, kn_nope_g, kn_rope_g, tm=512, hps=8):
    n = b * s
    tm = _tile(s, tm)
    spb = s // tm
    invf, sgn = _rope_consts()
    scale = float(QK_DIM) ** -0.5 * LOG2_E
    gq = jnp.concatenate([qn_nope_g, qn_rope_g, _swap_halves(qn_rope_g)]).reshape(1, HEAD_COLS)
    gkr = jnp.concatenate([kn_rope_g, _swap_halves(kn_rope_g)]).reshape(1, LANES)
    hps = _tile(heads, hps)
    row = lambda i, h: (i, 0)
    const = lambda i, h: (0, 0)
    head_out = lambda i, h: (i // spb, h, i % spb, 0)
    assert q_lora % kv_lora == 0 and kv_lora % LANES == 0
    ckv_blk = q_lora // kv_lora
    kr_blk = (q_lora + kv_lora) // LANES

    q = pl.pallas_call(
        functools.partial(_qproj_kernel, scale=scale),
        grid=(n // tm, heads // hps),
        in_specs=[pl.BlockSpec((tm, q_lora), row), pl.BlockSpec((tm, 1), row),
                  pl.BlockSpec((1, LANES), const), pl.BlockSpec((1, LANES), const),
                  pl.BlockSpec((1, q_lora), const), pl.BlockSpec((1, HEAD_COLS), const),
                  pl.BlockSpec((q_lora, hps * HEAD_COLS), lambda i, h: (0, h))],
        out_specs=pl.BlockSpec((None, hps, tm, QK_DIM), head_out),
        out_shape=jax.ShapeDtypeStruct((b, heads, s, QK_DIM), BF16),
        scratch_shapes=[pltpu.VMEM((tm, q_lora), BF16), pltpu.VMEM((tm, LANES), F32), pltpu.VMEM((tm, LANES), F32)],
        compiler_params=_params("parallel", "arbitrary"),
        name="mla_q_proj",
    )(small, pos, invf, sgn, q_lat_g.reshape(1, q_lora), gq, wq)

    k, v = pl.pallas_call(
        _kvproj_kernel,
        grid=(n // tm, heads // hps),
        in_specs=[pl.BlockSpec((tm, kv_lora), lambda i, h: (i, ckv_blk)),
                  pl.BlockSpec((tm, LANES), lambda i, h: (i, kr_blk)),
                  pl.BlockSpec((tm, 1), row),
                  pl.BlockSpec((1, LANES), const), pl.BlockSpec((1, LANES), const),
                  pl.BlockSpec((1, kv_lora), const), pl.BlockSpec((1, LANES), const),
                  pl.BlockSpec((1, NOPE_DIM), const),
                  pl.BlockSpec((kv_lora, hps * HEAD_COLS), lambda i, h: (0, h))],
        out_specs=[pl.BlockSpec((None, hps, tm, QK_DIM), head_out),
                   pl.BlockSpec((None, hps, tm, V_DIM), head_out)],
        out_shape=[jax.ShapeDtypeStruct((b, heads, s, QK_DIM), BF16),
                   jax.ShapeDtypeStruct((b, heads, s, V_DIM), BF16)],
        scratch_shapes=[pltpu.VMEM((tm, kv_lora), BF16), pltpu.VMEM((tm, LANES), BF16),
                        pltpu.VMEM((tm, LANES), F32), pltpu.VMEM((tm, LANES), F32)],
        compiler_params=_params("parallel", "arbitrary"),
        name="mla_kv_proj",
    )(small, small, pos, invf, sgn, kv_lat_g.reshape(1, kv_lora), gkr, kn_nope_g.reshape(1, NOPE_DIM), wkv)
    return q, k, v


def _flash_kernel(q_ref, k_ref, v_ref, o_ref, s_a, s_b, m_s, l_s, acc_s, *, tq, tk):
    n_q = q_ref.shape[0] // tq

    def scores(j, dst, q0, r0=0):
        ks = k_ref[pl.ds(pl.multiple_of(j * tk, tk), tk), :]
        dst[r0:, :] = lax.dot_general(q_ref[q0 + r0:q0 + tq, :], ks, (((1,), (1,)), ((), ())),
                                      preferred_element_type=F32)

    def soft_pv(src, j, q0, r0=0, diagonal=False):
        rows = slice(q0 + r0, q0 + tq)
        s = src[r0:, :]
        if diagonal:
            row = lax.broadcasted_iota(I32, s.shape, 0)
            col = lax.broadcasted_iota(I32, s.shape, 1)
            s = jnp.where(col <= row, s, -jnp.inf)
        m_prev = m_s[rows, :]
        m_new = jnp.maximum(m_prev, jnp.max(s, axis=-1, keepdims=True))
        alpha = jnp.exp2(m_prev - m_new)
        ps = [jnp.exp2(s[:, c * LANES:(c + 1) * LANES] - m_new) for c in range(tk // LANES)]
        part = ps[0]
        for x in ps[1:]:
            part = part + x
        l_s[rows, :] = alpha * l_s[rows, :] + part
        m_s[rows, :] = m_new
        vs = v_ref[pl.ds(pl.multiple_of(j * tk, tk), tk), :]
        p = jnp.concatenate(ps, axis=1).astype(BF16)
        acc_s[rows, :] = alpha * acc_s[rows, :] + jnp.dot(p, vs, preferred_element_type=F32)

    m_s[...] = jnp.full_like(m_s, -jnp.inf)
    l_s[...] = jnp.zeros_like(l_s)
    acc_s[...] = jnp.zeros_like(acc_s)

    for i in range(n_q):
        q0 = i * tq
        scores(0, s_a, q0)

        def pair(jj, c, q0=q0):
            j = 2 * jj
            scores(j + 1, s_b, q0)
            soft_pv(s_a, j, q0)
            scores(j + 2, s_a, q0)
            soft_pv(s_b, j + 1, q0)
            return c

        if i:
            lax.fori_loop(0, i, pair, 0)
        scores(2 * i + 1, s_b, q0, tk)
        soft_pv(s_a, 2 * i, q0, 0, True)
        soft_pv(s_b, 2 * i + 1, q0, tk, True)
        l = jnp.sum(l_s[q0:q0 + tq, :], axis=-1, keepdims=True)
        o_ref[q0:q0 + tq, :] = (acc_s[q0:q0 + tq, :] / l).astype(o_ref.dtype)


def _flash_attention(q, k, v, tq=1024):
    b, heads, s, _ = q.shape
    tq = _tile(s, tq)
    tk = tq // 2
    assert tk % LANES == 0
    whole = lambda bi, h: (bi, h, 0, 0)
    return pl.pallas_call(
        functools.partial(_flash_kernel, tq=tq, tk=tk),
        grid=(b, heads),
        in_specs=[pl.BlockSpec((None, None, s, QK_DIM), whole), pl.BlockSpec((None, None, s, QK_DIM), whole),
                  pl.BlockSpec((None, None, s, V_DIM), whole)],
        out_specs=pl.BlockSpec((s, V_DIM), lambda bi, h: (bi, h)),
        out_shape=jax.ShapeDtypeStruct((b * s, heads * V_DIM), BF16),
        scratch_shapes=[pltpu.VMEM((tq, tk), F32), pltpu.VMEM((tq, tk), F32),
                        pltpu.VMEM((s, LANES), F32), pltpu.VMEM((s, LANES), F32), pltpu.VMEM((s, V_DIM), F32)],
        compiler_params=_params("parallel", "parallel"),
        name="mla_flash_attention",
    )(q, k, v)


def _conv_kernel(cb_ref, cc_ref, ch_ref, pc_ref, ph_ref, w_ref, o_ref, *, tiles_per_seq):
    first = (pl.program_id(0) % tiles_per_seq) == 0
    u = cc_ref[...].astype(F32) * ch_ref[...].astype(F32)
    halo = pc_ref[...].astype(F32) * ph_ref[...].astype(F32)
    halo = jnp.where(first, 0.0, halo)
    cat = jnp.concatenate([halo, u], axis=0)
    n_halo = halo.shape[0]
    u1 = pltpu.roll(cat, 1, 0)[n_halo:]
    u2 = pltpu.roll(cat, 2, 0)[n_halo:]
    y = w_ref[0:1, :] * u2 + w_ref[1:2, :] * u1 + w_ref[2:3, :] * u
    o_ref[...] = (cb_ref[...].astype(F32) * y).astype(o_ref.dtype)


def _short_conv(big, conv_w, n, s, d, ts=512, tc=1024):
    ts, tc = _tile(s, ts), _tile(d, tc)
    halo = 8
    cpb = d // tc
    seg = lambda k: (lambda i, j: (i, k * cpb + j))
    prev = lambda k: (lambda i, j: (jnp.maximum(i * (ts // halo) - 1, 0), k * cpb + j))
    return pl.pallas_call(
        functools.partial(_conv_kernel, tiles_per_seq=s // ts),
        grid=(n // ts, cpb),
        in_specs=[pl.BlockSpec((ts, tc), seg(0)), pl.BlockSpec((ts, tc), seg(1)), pl.BlockSpec((ts, tc), seg(2)),
                  pl.BlockSpec((halo, tc), prev(1)), pl.BlockSpec((halo, tc), prev(2)),
                  pl.BlockSpec((conv_w.shape[0], tc), lambda i, j: (0, j))],
        out_specs=pl.BlockSpec((ts, tc), lambda i, j: (i, j)),
        out_shape=jax.ShapeDtypeStruct((n, d), BF16),
        compiler_params=_params("parallel", "parallel"),
        name="short_conv",
    )(big, big, big, big, big, conv_w)


def _memattn_kernel(q_ref, k_ref, v_ref, gq_ref, gk_ref, o_ref, kn_s, *, scale):
    @pl.when(pl.program_id(2) == 0)
    def _():
        k = k_ref[...].astype(F32)
        r = lax.rsqrt(jnp.mean(k * k, axis=-1, keepdims=True) + EPS)
        kn_s[...] = (k * r * gk_ref[...]).astype(BF16)

    q = q_ref[...].astype(F32)
    r = lax.rsqrt(jnp.mean(q * q, axis=-1, keepdims=True) + EPS)
    qn = (q * (r * scale) * gq_ref[...]).astype(BF16)
    s = lax.dot_general(qn, kn_s[...], (((1,), (1,)), ((), ())), preferred_element_type=F32)
    m = jnp.max(s, axis=-1, keepdims=True)
    p = jnp.exp(s - m)
    l = jnp.sum(p, axis=-1, keepdims=True)
    o = jnp.dot(p.astype(BF16), v_ref[...], preferred_element_type=F32)
    o_ref[...] = (o / l).astype(o_ref.dtype)


def _memory_attention(big, kvm, gq, gk, b, s, d, mem_len, q_seg, ts=512):
    hd = d // MEM_HEADS
    ts = _tile(s, ts)
    spb = s // ts
    return pl.pallas_call(
        functools.partial(_memattn_kernel, scale=float(hd) ** -0.5),
        grid=(b, MEM_HEADS, spb),
        in_specs=[pl.BlockSpec((ts, hd), lambda bi, h, i: (bi * spb + i, q_seg * MEM_HEADS + h)),
                  pl.BlockSpec((mem_len, hd), lambda bi, h, i: (bi, h)),
                  pl.BlockSpec((mem_len, hd), lambda bi, h, i: (bi, MEM_HEADS + h)),
                  pl.BlockSpec((1, hd), lambda bi, h, i: (0, 0)),
                  pl.BlockSpec((1, hd), lambda bi, h, i: (0, 0))],
        out_specs=pl.BlockSpec((ts, hd), lambda bi, h, i: (bi * spb + i, h)),
        out_shape=jax.ShapeDtypeStruct((b * s, d), BF16),
        scratch_shapes=[pltpu.VMEM((mem_len, hd), BF16)],
        compiler_params=_params("parallel", "parallel", "arbitrary"),
        name="memory_attention",
    )(big, kvm, kvm, gq.reshape(1, hd), gk.reshape(1, hd))


def _merge_kernel(a0_ref, a1_ref, a2_ref, w_ref, g_ref, o_ref, acc_s):
    br = pl.program_id(2)
    gate = jax.nn.sigmoid(g_ref[...].astype(F32))

    @pl.when(br == 0)
    def _():
        acc_s[...] = gate * jnp.dot(a0_ref[...], w_ref[...], preferred_element_type=F32)

    @pl.when(br == 1)
    def _():
        acc_s[...] += gate * jnp.dot(a1_ref[...], w_ref[...], preferred_element_type=F32)

    @pl.when(br == 2)
    def _():
        o_ref[...] = (acc_s[...] + gate * jnp.dot(a2_ref[...], w_ref[...], preferred_element_type=F32)).astype(o_ref.dtype)


def _gated_merge(a_mla, a_conv, a_mem, w_branch, big, gate_seg, tm=512, tn=1024):
    n, d = a_mla.shape
    tm, tn = _tile(n, tm), _tile(d, tn)
    cpb = d // tn
    lhs = pl.BlockSpec((tm, d), lambda i, j, br: (i, 0))
    return pl.pallas_call(
        _merge_kernel,
        grid=(n // tm, cpb, 3),
        in_specs=[lhs, lhs, lhs,
                  pl.BlockSpec((None, d, tn), lambda i, j, br: (br, 0, j)),
                  pl.BlockSpec((tm, tn), lambda i, j, br: (i, (gate_seg + br) * cpb + j))],
        out_specs=pl.BlockSpec((tm, tn), lambda i, j, br: (i, j)),
        out_shape=jax.ShapeDtypeStruct((n, d), BF16),
        scratch_shapes=[pltpu.VMEM((tm, tn), F32)],
        compiler_params=_params("parallel", "parallel", "arbitrary"),
        name="gated_merge",
    )(a_mla, a_conv, a_mem, w_branch, big)


def _router_kernel(x_ref, g_ref, wr_ref, br_ref, hp_ref, idx_ref, wt_ref, rank_ref, cnt_ref, carry_s, *, n_exp):
    tm = x_ref.shape[0]

    @pl.when(pl.program_id(0) == 0)
    def _():
        carry_s[...] = jnp.zeros_like(carry_s)

    x = x_ref[...]
    r = lax.rsqrt(jnp.mean(x * x, axis=-1, keepdims=True) + EPS)
    h = x * r * g_ref[...]
    hp_ref[...] = _pack_halves(h)

    logits = jnp.dot(h, wr_ref[...], preferred_element_type=F32, precision=lax.Precision.HIGHEST) + br_ref[...]
    lane = lax.broadcasted_iota(I32, (tm, n_exp), 1)
    out_lane = lax.broadcasted_iota(I32, (tm, LANES), 1)
    work = logits
    vals, hots = [], []
    idx_out = jnp.zeros((tm, LANES), I32)
    for k in range(TOP_K):
        m = jnp.max(work, axis=-1, keepdims=True)
        idx = jnp.min(jnp.where(work == m, lane, n_exp), axis=-1, keepdims=True)
        hot = lane == idx
        vals.append(m)
        hots.append(hot)
        idx_out = jnp.where(out_lane == k, idx, idx_out)
        work = jnp.where(hot, -jnp.inf, work)
    idx_ref[...] = idx_out

    exps = [jnp.exp(v - vals[0]) for v in vals]
    denom = exps[0]
    for e in exps[1:]:
        denom = denom + e
    wt_out = jnp.zeros((tm, LANES), F32)
    for k in range(TOP_K):
        wt_out = jnp.where(out_lane == k, exps[k] / denom, wt_out)
    wt_ref[...] = wt_out

    member = jnp.zeros((tm, n_exp), F32)
    for hot in hots:
        member = member + hot.astype(F32)
    tri = (lax.broadcasted_iota(I32, (tm, tm), 1) < lax.broadcasted_iota(I32, (tm, tm), 0)).astype(BF16)
    prefix = jnp.dot(tri, member.astype(BF16), preferred_element_type=F32) + carry_s[0:1, 0:n_exp]
    rank_out = jnp.zeros((tm, LANES), I32)
    for k in range(TOP_K):
        rk = jnp.sum(jnp.where(hots[k], prefix, 0.0), axis=-1, keepdims=True).astype(I32)
        rank_out = jnp.where(out_lane == k, rk, rank_out)
    rank_ref[...] = rank_out

    carry_s[0:1, 0:n_exp] = carry_s[0:1, 0:n_exp] + jnp.sum(member, axis=0, keepdims=True)
    cnt_ref[...] = carry_s[...]


def _router(x1, g, w_router, b_router, tm=512):
    n, d = x1.shape
    n_exp = w_router.shape[1]
    assert n_exp <= LANES
    tm = _tile(n, tm)
    row = lambda i: (i, 0)
    const = lambda i: (0, 0)
    return pl.pallas_call(
        functools.partial(_router_kernel, n_exp=n_exp),
        grid=(n // tm,),
        in_specs=[pl.BlockSpec((tm, d), row), pl.BlockSpec((1, d), const),
                  pl.BlockSpec((d, n_exp), const), pl.BlockSpec((1, n_exp), const)],
        out_specs=[pl.BlockSpec((tm, d // 2), row), pl.BlockSpec((tm, LANES), row), pl.BlockSpec((tm, LANES), row),
                   pl.BlockSpec((tm, LANES), row), pl.BlockSpec((8, LANES), const)],
        out_shape=[jax.ShapeDtypeStruct((n, d // 2), U32), jax.ShapeDtypeStruct((n, LANES), I32),
                   jax.ShapeDtypeStruct((n, LANES), F32), jax.ShapeDtypeStruct((n, LANES), I32),
                   jax.ShapeDtypeStruct((8, LANES), F32)],
        scratch_shapes=[pltpu.VMEM((8, LANES), F32)],
        compiler_params=_params("arbitrary"),
        name="router",
    )(x1, g.reshape(1, d), w_router, b_router.reshape(1, n_exp))


def _dispatch_kernel(dest_ref, zflag_ref, hp_ref, xs_ref, zeros_s, sem, zsem, *, rows):
    base = pl.program_id(0) * rows
    te = zeros_s.shape[0]

    @pl.when(pl.program_id(0) == 0)
    def _():
        zeros_s[...] = jnp.zeros_like(zeros_s)

        def fill(t):
            return pltpu.make_async_copy(zeros_s, xs_ref.at[pl.ds(pl.multiple_of(t * te, te), te)], zsem)

        def start(t, c):
            @pl.when(zflag_ref[t] != 0)
            def _():
                fill(t).start()
            return c

        def finish(t, c):
            @pl.when(zflag_ref[t] != 0)
            def _():
                fill(t).wait()
            return c

        lax.fori_loop(0, zflag_ref.shape[0], start, 0)
        lax.fori_loop(0, zflag_ref.shape[0], finish, 0)

    def issue(r, c):
        for k in range(TOP_K):
            d = dest_ref[(base + r) * TOP_K + k]
            pltpu.make_async_copy(hp_ref.at[pl.ds(r, 1)], xs_ref.at[pl.ds(d, 1)], sem).start(priority=k % 2)
        return c

    lax.fori_loop(0, rows, issue, 0)

    def drain(r, c):
        for k in range(TOP_K):
            pltpu.make_async_copy(hp_ref.at[pl.ds(0, 1)], xs_ref.at[pl.ds(0, 1)], sem).wait()
        return c

    lax.fori_loop(0, rows, drain, 0)


def _dispatch(dest_flat, zero_flag, hp, te, rows=256):
    n, half = hp.shape
    rows = _tile(n, rows)
    n_slots = zero_flag.shape[0] * te
    return pl.pallas_call(
        functools.partial(_dispatch_kernel, rows=rows),
        grid_spec=pltpu.PrefetchScalarGridSpec(
            num_scalar_prefetch=2,
            grid=(n // rows,),
            in_specs=[pl.BlockSpec((rows, half), lambda i, dest, zflag: (i, 0))],
            out_specs=pl.BlockSpec(memory_space=pl.ANY),
            scratch_shapes=[pltpu.VMEM((te, half), U32), pltpu.SemaphoreType.DMA(()), pltpu.SemaphoreType.DMA(())],
        ),
        out_shape=jax.ShapeDtypeStruct((n_slots, half), U32),
        compiler_params=_params("arbitrary"),
        name="expert_dispatch",
    )(dest_flat, zero_flag, hp)


def _group_items(tile_e, first_tile, tiles_of, rows_of, n_used, n_blocks, n_tiles, te):
    n_items = n_used * n_blocks
    w = jnp.arange(n_tiles * n_blocks, dtype=I32)
    wc = jnp.minimum(w, n_items - 1)
    e = tile_e[wc // n_blocks]
    first, count = first_tile[e], jnp.maximum(tiles_of[e], 1)
    r = wc - first * n_blocks
    in_tile, in_blk = first + r % count, r // count
    z = jnp.maximum(w - n_items, 0)
    live = w < n_items
    out_tile = jnp.where(live, in_tile, n_used + z // n_blocks)
    out_blk = jnp.where(live, in_blk, z % n_blocks)
    starts = (live & (r % count == 0)).astype(I32)
    real = rows_of[e] - (in_tile - first) * te
    fill = jnp.where(real <= 0, 2, jnp.where(real <= te // 2, 1, 0)).astype(I32)
    return in_tile, in_blk, e, starts, fill, out_tile, out_blk, n_items.reshape(1)


def _expert_up_kernel(tile_ref, blk_ref, exp_ref, start_ref, fill_ref, ot_ref, ob_ref, ni_ref,
                      xs_ref, wg_ref, bg_ref, wu_ref, bu_ref, act_ref, xb_s, wg_s, wu_s):
    w = pl.program_id(0)
    live = w < ni_ref[0]
    te, half = xs_ref.shape

    @pl.when(start_ref[w] != 0)
    def _():
        wg_s[...] = wg_ref[...].astype(BF16)
        wu_s[...] = wu_ref[...].astype(BF16)

    def compute(rows):
        lo, hi = _unpack_halves(xs_ref[:rows, :])
        xb_s[:rows, :half] = lo.astype(BF16)
        xb_s[:rows, half:] = hi.astype(BF16)
        xb = xb_s[:rows, :]
        g = jnp.dot(xb, wg_s[...], preferred_element_type=F32) + bg_ref[...]
        u = jnp.dot(xb, wu_s[...], preferred_element_type=F32) + bu_ref[...]
        g = jnp.minimum(g, SWIGLU_LIMIT)
        u = jnp.clip(u, -SWIGLU_LIMIT, SWIGLU_LIMIT)
        act_ref[:rows, :] = (g * jax.nn.sigmoid(SWIGLU_ALPHA * g) * (u + 1.0)).astype(act_ref.dtype)
        if rows < te:
            act_ref[rows:, :] = jnp.zeros((te - rows, act_ref.shape[1]), act_ref.dtype)

    @pl.when(jnp.logical_and(live, fill_ref[w] == 0))
    def _():
        compute(te)

    @pl.when(jnp.logical_and(live, fill_ref[w] == 1))
    def _():
        compute(te // 2)

    @pl.when(jnp.logical_or(jnp.logical_not(live), fill_ref[w] == 2))
    def _():
        act_ref[...] = jnp.zeros_like(act_ref)


def _expert_down_kernel(tile_ref, blk_ref, exp_ref, start_ref, fill_ref, ot_ref, ob_ref, ni_ref,
                        act_ref, wdl_ref, wdh_ref, bdl_ref, bdh_ref, o_ref, wl_s, wh_s):
    w = pl.program_id(0)
    live = w < ni_ref[0]
    te = act_ref.shape[0]

    @pl.when(start_ref[w] != 0)
    def _():
        wl_s[...] = wdl_ref[...].astype(BF16)
        wh_s[...] = wdh_ref[...].astype(BF16)

    def compute(rows):
        a = act_ref[:rows, :]
        lo = jnp.dot(a, wl_s[...], preferred_element_type=F32) + bdl_ref[...]
        hi = jnp.dot(a, wh_s[...], preferred_element_type=F32) + bdh_ref[...]
        o_ref[:rows, :] = _pack_pair(lo, hi)
        if rows < te:
            o_ref[rows:, :] = jnp.zeros((te - rows, o_ref.shape[1]), o_ref.dtype)

    @pl.when(jnp.logical_and(live, fill_ref[w] == 0))
    def _():
        compute(te)

    @pl.when(jnp.logical_and(live, fill_ref[w] == 1))
    def _():
        compute(te // 2)

    @pl.when(jnp.logical_or(jnp.logical_not(live), fill_ref[w] == 2))
    def _():
        o_ref[...] = jnp.zeros_like(o_ref)


def _expert_mlp(tile_e, first_tile, tiles_of, rows_of, n_used, xs, wg, bg, wu, bu, wd, bd, te,
                up_rows=256, tcu=512, tnd=1024):
    n_slots, half = xs.shape
    d = 2 * half
    n_exp, _, de = wg.shape
    tcu, tnd = _tile(de, tcu), _tile(half, tnd)
    n_up, n_down = de // tcu, half // tnd
    n_tiles = n_slots // te
    up_rows = _tile(te, up_rows)
    sub = te // up_rows
    n_pre = 8

    in_row = lambda w, tl, bk, ex, st, sh, ot, ob, ni: (tl[w], 0)
    w_blk = lambda w, tl, bk, ex, st, sh, ot, ob, ni: (ex[w], 0, bk[w])
    out_blk = lambda w, tl, bk, ex, st, sh, ot, ob, ni: (ot[w], ob[w])

    up_items = _group_items(jnp.repeat(tile_e, sub), first_tile * sub, tiles_of * sub, rows_of, n_used * sub,
                            n_up, n_tiles * sub, up_rows)
    act = pl.pallas_call(
        _expert_up_kernel,
        grid_spec=pltpu.PrefetchScalarGridSpec(
            num_scalar_prefetch=n_pre,
            grid=(n_tiles * sub * n_up,),
            in_specs=[pl.BlockSpec((up_rows, half), in_row),
                      pl.BlockSpec((None, d, tcu), w_blk), pl.BlockSpec((None, 1, tcu), w_blk),
                      pl.BlockSpec((None, d, tcu), w_blk), pl.BlockSpec((None, 1, tcu), w_blk)],
            out_specs=pl.BlockSpec((up_rows, tcu), out_blk),
            scratch_shapes=[pltpu.VMEM((up_rows, d), BF16), pltpu.VMEM((d, tcu), BF16), pltpu.VMEM((d, tcu), BF16)],
        ),
        out_shape=jax.ShapeDtypeStruct((n_slots, de), BF16),
        compiler_params=_params("arbitrary"),
        name="expert_gate_up",
    )(*up_items, xs, wg, bg.reshape(n_exp, 1, de), wu, bu.reshape(n_exp, 1, de))

    hi_blk = lambda w, tl, bk, ex, st, sh, ot, ob, ni: (ex[w], 0, n_down + bk[w])
    bd3 = bd.reshape(n_exp, 1, d)
    down_items = _group_items(tile_e, first_tile, tiles_of, rows_of, n_used, n_down, n_tiles, te)
    return pl.pallas_call(
        _expert_down_kernel,
        grid_spec=pltpu.PrefetchScalarGridSpec(
            num_scalar_prefetch=n_pre,
            grid=(n_tiles * n_down,),
            in_specs=[pl.BlockSpec((te, de), in_row),
                      pl.BlockSpec((None, de, tnd), w_blk), pl.BlockSpec((None, de, tnd), hi_blk),
                      pl.BlockSpec((None, 1, tnd), w_blk), pl.BlockSpec((None, 1, tnd), hi_blk)],
            out_specs=pl.BlockSpec((te, tnd), out_blk),
            scratch_shapes=[pltpu.VMEM((de, tnd), BF16), pltpu.VMEM((de, tnd), BF16)],
        ),
        out_shape=jax.ShapeDtypeStruct((n_slots, half), U32),
        compiler_params=_params("arbitrary"),
        name="expert_down",
    )(*down_items, act, wd, wd, bd3, bd3)


def _combine_kernel(dest_ref, ys_ref, x_ref, wt_ref, o_ref, gbuf, sem, *, rows):
    base = pl.program_id(0) * rows
    half = gbuf.shape[2]

    def issue(r, c):
        for k in range(TOP_K):
            d = dest_ref[(base + r) * TOP_K + k]
            pltpu.make_async_copy(ys_ref.at[pl.ds(d, 1)], gbuf.at[k, pl.ds(r, 1)], sem).start(priority=k % 2)
        return c

    lax.fori_loop(0, rows, issue, 0)

    def drain(r, c):
        for k in range(TOP_K):
            pltpu.make_async_copy(ys_ref.at[pl.ds(0, 1)], gbuf.at[k, pl.ds(0, 1)], sem).wait()
        return c

    lax.fori_loop(0, rows, drain, 0)

    lo_sum = x_ref[:, :half]
    hi_sum = x_ref[:, half:]
    for k in range(TOP_K):
        lo, hi = _unpack_halves(gbuf[k])
        wk = wt_ref[:, k:k + 1]
        lo_sum = lo_sum + wk * lo
        hi_sum = hi_sum + wk * hi
    o_ref[:, :half] = lo_sum
    o_ref[:, half:] = hi_sum


def _combine(dest_flat, ys, x1, wt, rows=256):
    n, d = x1.shape
    rows = _tile(n, rows)
    return pl.pallas_call(
        functools.partial(_combine_kernel, rows=rows),
        grid_spec=pltpu.PrefetchScalarGridSpec(
            num_scalar_prefetch=1,
            grid=(n // rows,),
            in_specs=[pl.BlockSpec(memory_space=pl.ANY),
                      pl.BlockSpec((rows, d), lambda i, dest: (i, 0)),
                      pl.BlockSpec((rows, LANES), lambda i, dest: (i, 0))],
            out_specs=pl.BlockSpec((rows, d), lambda i, dest: (i, 0)),
            scratch_shapes=[pltpu.VMEM((TOP_K, rows, d // 2), U32), pltpu.SemaphoreType.DMA(())],
        ),
        out_shape=jax.ShapeDtypeStruct((n, d), F32),
        compiler_params=_params("arbitrary"),
        name="expert_combine",
    )(dest_flat, ys, x1, wt)


def _routed_experts(x1, norm_g, w_router, b_router, wg, bg, wu, bu, wd, bd, te=512):
    n, d = x1.shape
    n_exp = w_router.shape[1]
    hp, idx_p, wt_p, rank_p, cnt = _router(x1, norm_g, w_router, b_router)
    top_idx = idx_p[:, :TOP_K]
    rank = rank_p[:, :TOP_K]

    te = min(te, n * TOP_K)
    counts = cnt[0, :n_exp].astype(I32)
    padded = ((counts + te - 1) // te) * te
    pend = jnp.cumsum(padded)
    pstart = pend - padded
    n_tiles = (n * TOP_K) // te + n_exp
    n_used = (pend[-1] // te).astype(I32)
    tiles = jnp.arange(n_tiles, dtype=I32)
    tile_e = jnp.minimum(jnp.sum((pend[None, :] <= (tiles * te)[:, None]).astype(I32), axis=1), n_exp - 1)
    dest = (pstart[top_idx] + rank).reshape(n * TOP_K).astype(I32)
    partial_last = jnp.any((tiles[:, None] == (pend // te - 1)[None, :]) & ((counts % te) != 0)[None, :], axis=1)
    zero_flag = (partial_last | (tiles >= n_used)).astype(I32)

    xs = _dispatch(dest, zero_flag, hp, te)
    ys = _expert_mlp(tile_e, pstart // te, padded // te, counts, n_used, xs, wg, bg, wu, bu, wd, bd, te)
    return _combine(dest, ys, x1, wt_p)


def kernel(x, mem, positions, norm1_g, w_in, q_lat_g, kv_lat_g, w_uq, w_ukv, qn_nope_g, qn_rope_g, kn_nope_g, kn_rope_g, conv_w, mem_norm_g, w_mem_kv, mem_qn_g, mem_kn_g, w_branch, w_out, norm2_g, w_router, b_router, w_gate, b_gate, w_up, b_up, w_down, b_down):
    b, s, d = x.shape
    n = b * s
    depth = w_in.shape[0]
    q_lora = w_uq.shape[1]
    kv_lora = w_ukv.shape[1]
    heads = w_ukv.shape[2] // (NOPE_DIM + V_DIM)
    mem_len = mem.shape[1]
    assert heads * V_DIM == d and d % (2 * LANES) == 0
    lat = q_lora + kv_lora
    kr_end = lat + ROPE_DIM
    assert w_in.shape[2] == kr_end + 7 * d

    pos = positions.reshape(n, 1).astype(I32)
    xf = x.reshape(n, d)
    mem_f = mem.reshape(b * mem_len, d)

    for l in range(depth):
        w_l = w_in[l]
        w_t = jnp.swapaxes(w_l, 0, 1).astype(BF16)
        w_small_t = jnp.concatenate(
            [w_t[:kr_end], w_t[lat + ROPE_DIM // 2:kr_end], w_t[lat:lat + ROPE_DIM // 2]], axis=0)
        wq3 = w_uq[l].reshape(q_lora, heads, QK_DIM)
        wq = jnp.concatenate([wq3, wq3[..., NOPE_DIM + ROPE_DIM // 2:], wq3[..., NOPE_DIM:NOPE_DIM + ROPE_DIM // 2]],
                             axis=-1).reshape(q_lora, heads * HEAD_COLS).astype(BF16)
        wkv = w_ukv[l].astype(BF16)

        xn = _rmsnorm(xf, norm1_g[l])
        big = _matmul_nt(xn, w_t, BF16, row0=kr_end, n=7 * d, name="in_proj_wide")
        small = _matmul_nt(xn, w_small_t, BF16, tn=w_small_t.shape[0], name="in_proj_latent")

        q, k, v = _mla_projections(small, pos, b, s, heads, q_lora, kv_lora, q_lat_g[l], kv_lat_g[l], wq, wkv,
                                   qn_nope_g[l], qn_rope_g[l], kn_nope_g[l], kn_rope_g[l])
        a_mla = _flash_attention(q, k, v)

        a_conv = _short_conv(big, conv_w[l], n, s, d)

        mem_n = _rmsnorm(mem_f, mem_norm_g[l])
        kvm = _matmul(mem_n, w_mem_kv[l].astype(BF16), BF16, name="mem_kv_proj")
        a_mem = _memory_attention(big, kvm, mem_qn_g[l], mem_kn_g[l], b, s, d, mem_len, q_seg=3)

        merged = _gated_merge(a_mla, a_conv, a_mem, w_branch[l].astype(BF16), big, gate_seg=4)
        x1 = _matmul(merged, w_out[l].astype(BF16), F32, res=xf, name="out_proj")

        xf = _routed_experts(x1, norm2_g[l], w_router[l], b_router[l],
                             w_gate[l], b_gate[l], w_up[l], b_up[l], w_down[l], b_down[l])
    return xf.reshape(b, s, d)
```
